```python
import jax
import jax.numpy as jnp
from jax import lax

D_MODEL = 1024
BATCH = 8
SEQ = 4096
DEPTH = 4

MEM_LEN = 256
N_MIXERS = 3
HEAD_DIM = 64
N_MIX_HEADS = 12
MIX_WIDTH = N_MIX_HEADS * HEAD_DIM
N_MEM_HEADS = 4
MEM_WIDTH = N_MEM_HEADS * HEAD_DIM
OUT_WIDTH = MIX_WIDTH + MEM_WIDTH
Q_BLOCK = 128
MLA_Q_RANK = 256
MLA_KV_RANK = 128
MLA_NOPE_DIM = 64
MLA_ROPE_DIM = 32
MLA_QK_DIM = MLA_NOPE_DIM + MLA_ROPE_DIM
MLA_V_DIM = 64
ROPE_THETA = 10000.0
MOBA_BLOCK = 256
MOBA_TOPK = 3
D_FF = 2816
CONV_WIDTH = 3
EPS = 1e-6
POS_OFFSET_MAX = 1024

FOX_IN = 3 * MIX_WIDTH + N_MIX_HEADS + MEM_WIDTH
MLA_IN = MLA_Q_RANK + MLA_KV_RANK + MLA_ROPE_DIM + MEM_WIDTH
MOBA_IN = 3 * MIX_WIDTH + MEM_WIDTH

kernel_name = 'hybrid_fox_mla_moba_memxattn_convffn'


def rmsnorm(x, g):
    xf = x.astype(jnp.float32)
    y = xf * lax.rsqrt(jnp.mean(xf * xf, axis=-1, keepdims=True) + EPS)
    return (y * g.astype(jnp.float32)).astype(x.dtype)


def alibi_slopes(n_heads):
    return 2.0 ** (-8.0 * jnp.arange(1, n_heads + 1, dtype=jnp.float32) / n_heads)


def rope(x, cos, sin):
    half = x.shape[-1] // 2
    x1, x2 = x[..., :half], x[..., half:]
    return jnp.concatenate([x1 * cos - x2 * sin, x1 * sin + x2 * cos], axis=-1)


def to_heads(t, n_heads):
    b, s, _ = t.shape
    return t.reshape(b, s, n_heads, -1).transpose(0, 2, 1, 3)


def from_heads(t):
    b, h, s, d = t.shape
    return t.transpose(0, 2, 1, 3).reshape(b, s, h * d)


def causal_block_attention(q, k, v, log_decay=None):
    b, h, s, dk = q.shape
    nq = s // Q_BLOCK
    q_chunks = q.reshape(b, h, nq, Q_BLOCK, dk).transpose(2, 0, 1, 3, 4)
    key_idx = jnp.arange(s)
    xs = [q_chunks, jnp.arange(nq)]
    if log_decay is not None:
        xs.append(log_decay.reshape(b, h, nq, Q_BLOCK).transpose(2, 0, 1, 3))

    def step(args):
        qc, i = args[0], args[1]
        logits = jnp.einsum('bhqd,bhkd->bhqk', qc, k).astype(jnp.float32)
        if log_decay is not None:
            logits = logits + args[2][..., None] - log_decay[:, :, None, :]
        q_idx = i * Q_BLOCK + jnp.arange(Q_BLOCK)
        logits = jnp.where(key_idx[None, :] <= q_idx[:, None], logits, -jnp.inf)
        p = jax.nn.softmax(logits, axis=-1).astype(v.dtype)
        return jnp.einsum('bhqk,bhkd->bhqd', p, v)

    out = lax.map(step, tuple(xs))
    return out.transpose(1, 2, 0, 3, 4).reshape(b, h, s, v.shape[-1])


def fox_mixer(p, b_f, q_gain, k_gain):
    q, k, v, f = jnp.split(p, [MIX_WIDTH, 2 * MIX_WIDTH, 3 * MIX_WIDTH], axis=-1)
    q = rmsnorm(to_heads(q, N_MIX_HEADS), q_gain) * HEAD_DIM ** -0.5
    k = rmsnorm(to_heads(k, N_MIX_HEADS), k_gain)
    log_f = jax.nn.log_sigmoid(f.astype(jnp.float32) + b_f.astype(jnp.float32))
    cum = lax.cumsum(log_f, axis=1).transpose(0, 2, 1)
    return from_heads(causal_block_attention(q, k, to_heads(v, N_MIX_HEADS), cum))


def mla_mixer(p, cos, sin, qa_norm, kva_norm, w_q_up, w_kv_up, q_gain, k_gain):
    b, s, _ = p.shape
    q_lat, kv_lat, k_r = jnp.split(p, [MLA_Q_RANK, MLA_Q_RANK + MLA_KV_RANK], axis=-1)
    q = (rmsnorm(q_lat, qa_norm) @ w_q_up).reshape(b, s, N_MIX_HEADS, MLA_QK_DIM)
    kv = (rmsnorm(kv_lat, kva_norm) @ w_kv_up).reshape(b, s, N_MIX_HEADS, MLA_NOPE_DIM + MLA_V_DIM)
    q = jnp.concatenate([q[..., :MLA_NOPE_DIM], rope(q[..., MLA_NOPE_DIM:], cos, sin)], axis=-1)
    k_rope = jnp.broadcast_to(rope(k_r[:, :, None, :], cos, sin), (b, s, N_MIX_HEADS, MLA_ROPE_DIM))
    k = jnp.concatenate([kv[..., :MLA_NOPE_DIM], k_rope], axis=-1)
    v = kv[..., MLA_NOPE_DIM:]
    q = rmsnorm(q, q_gain).transpose(0, 2, 1, 3) * MLA_QK_DIM ** -0.5
    k = rmsnorm(k, k_gain).transpose(0, 2, 1, 3)
    return from_heads(causal_block_attention(q, k, v.transpose(0, 2, 1, 3)))


def moba_attention(q, k, v, positions, slopes):
    b, h, s, d = q.shape
    nb = -(-s // MOBA_BLOCK)
    pad = nb * MOBA_BLOCK - s
    kb = jnp.pad(k, ((0, 0), (0, 0), (0, pad), (0, 0))).reshape(b, h, nb, MOBA_BLOCK, d)
    vb = jnp.pad(v, ((0, 0), (0, 0), (0, pad), (0, 0))).reshape(b, h, nb, MOBA_BLOCK, d)
    posb = jnp.pad(positions, ((0, 0), (0, pad))).reshape(b, nb, MOBA_BLOCK)
    n_sel = min(MOBA_TOPK, nb - 1)
    nq = s // Q_BLOCK

    def chunks(t):
        tail = t.shape[3:]
        t = t.reshape((b, h, nq, Q_BLOCK) + tail)
        perm = (0, 2, 1, 3) + tuple(range(4, t.ndim))
        return t.transpose(perm).reshape((b * nq, h, Q_BLOCK) + tail)

    xs = [chunks(q), positions.reshape(b * nq, Q_BLOCK),
          jnp.repeat(jnp.arange(b), nq), jnp.tile(jnp.arange(nq), b)]
    if n_sel > 0:
        k_mean = kb.mean(axis=3)
        gate = jnp.einsum('bhsd,bhnd->bhsn', q, k_mean).astype(jnp.float32)
        q_blk = jnp.arange(s) // MOBA_BLOCK
        gate = jnp.where(jnp.arange(nb)[None, :] < q_blk[:, None], gate, -jnp.inf)
        g_val, g_idx = lax.top_k(gate, n_sel)
        xs += [chunks(g_idx), chunks(jnp.isfinite(g_val))]

    def step(args):
        qc, pq, bi, ci = args[0], args[1], args[2], args[3]
        t_idx = ci * Q_BLOCK + jnp.arange(Q_BLOCK)
        own = (ci * Q_BLOCK) // MOBA_BLOCK
        kb_b, vb_b, posb_b = kb[bi], vb[bi], posb[bi]
        k_own = lax.dynamic_index_in_dim(kb_b, own, axis=1, keepdims=False)
        v_own = lax.dynamic_index_in_dim(vb_b, own, axis=1, keepdims=False)
        pos_own = lax.dynamic_index_in_dim(posb_b, own, axis=0, keepdims=False)
        dist_own = (pq[:, None] - pos_own[None, :]).astype(jnp.float32)
        s_own = (jnp.einsum('hqd,hjd->hqj', qc, k_own).astype(jnp.float32)
                 - slopes[:, None, None] * dist_own[None])
        own_idx = own * MOBA_BLOCK + jnp.arange(MOBA_BLOCK)
        s_own = jnp.where(own_idx[None, None, :] <= t_idx[None, :, None], s_own, -jnp.inf)
        if n_sel == 0:
            p_own = jax.nn.softmax(s_own, axis=-1).astype(v.dtype)
            return jnp.einsum('hqj,hjd->hqd', p_own, v_own)
        idx, ok = args[4], args[5]
        k_sel = jax.vmap(lambda blocks, i: blocks[i])(kb_b, idx)
        v_sel = jax.vmap(lambda blocks, i: blocks[i])(vb_b, idx)
        dist_sel = (pq[None, :, None, None] - posb_b[idx]).astype(jnp.float32)
        s_sel = (jnp.einsum('hqd,hqkjd->hqkj', qc, k_sel).astype(jnp.float32)
                 - slopes[:, None, None, None] * dist_sel)
        s_sel = jnp.where(ok[..., None], s_sel, -jnp.inf).reshape(h, Q_BLOCK, n_sel * MOBA_BLOCK)
        p = jax.nn.softmax(jnp.concatenate([s_sel, s_own], axis=-1), axis=-1).astype(v.dtype)
        p_sel = p[..., :n_sel * MOBA_BLOCK].reshape(h, Q_BLOCK, n_sel, MOBA_BLOCK)
        p_own = p[..., n_sel * MOBA_BLOCK:]
        return (jnp.einsum('hqkj,hqkjd->hqd', p_sel, v_sel)
                + jnp.einsum('hqj,hjd->hqd', p_own, v_own))

    out = lax.map(step, tuple(xs))
    return out.reshape(b, nq, h, Q_BLOCK, d).transpose(0, 2, 1, 3, 4).reshape(b, h, s, d)


def moba_mixer(p, positions, q_gain, k_gain):
    q, k, v = jnp.split(p, [MIX_WIDTH, 2 * MIX_WIDTH], axis=-1)
    q = rmsnorm(to_heads(q, N_MIX_HEADS), q_gain) * HEAD_DIM ** -0.5
    k = rmsnorm(to_heads(k, N_MIX_HEADS), k_gain)
    out = moba_attention(q, k, to_heads(v, N_MIX_HEADS), positions, alibi_slopes(N_MIX_HEADS))
    return from_heads(out)


def memory_cross(qm, k_mem, v_mem, q_gain, k_gain):
    b, s, _ = qm.shape
    q = rmsnorm(qm.reshape(b, s, N_MEM_HEADS, HEAD_DIM), q_gain) * HEAD_DIM ** -0.5
    k = rmsnorm(k_mem, k_gain)
    logits = jnp.einsum('bshd,bmhd->bhsm', q, k).astype(jnp.float32)
    p = jax.nn.softmax(logits, axis=-1).astype(v_mem.dtype)
    return jnp.einsum('bhsm,bmhd->bshd', p, v_mem).reshape(b, s, MEM_WIDTH)


def conv_ffn(h, w_up, conv_w, conv_b, w_down):
    u = h @ w_up
    c = u.shape[-1]
    u = lax.conv_general_dilated(u, conv_w[:, None, :].astype(u.dtype), window_strides=(1,),
                                 padding=[(CONV_WIDTH - 1, 0)],
                                 dimension_numbers=('NWC', 'WIO', 'NWC'),
                                 feature_group_count=c) + conv_b
    gate, val = jnp.split(u, 2, axis=-1)
    return (jax.nn.silu(gate) * val) @ w_down


def setup_inputs(seed: int = 0) -> dict:
    key = jax.random.key(seed)
    ks = iter(jax.random.split(key, 32))
    f32 = jnp.float32

    def dense(shape):
        return jax.random.normal(next(ks), shape, f32) * shape[-2] ** -0.5

    def gain(shape):
        return 1.0 + 0.05 * jax.random.normal(next(ks), shape, f32)

    n_fox = len(range(0, DEPTH, N_MIXERS))
    n_mla = len(range(1, DEPTH, N_MIXERS))
    n_moba = len(range(2, DEPTH, N_MIXERS))
    x = jax.random.normal(next(ks), (BATCH, SEQ, D_MODEL), f32)
    mem = jax.random.normal(next(ks), (BATCH, MEM_LEN, D_MODEL), f32)
    positions = (jnp.arange(SEQ, dtype=jnp.int32)[None, :]
                 + jax.random.randint(next(ks), (BATCH, 1), 0, POS_OFFSET_MAX, dtype=jnp.int32))
    return {
        'x': x,
        'mem': mem,
        'positions': positions,
        'norm_mix': gain((DEPTH, D_MODEL)),
        'norm_ffn': gain((DEPTH, D_MODEL)),
        'w_out': dense((DEPTH, OUT_WIDTH, D_MODEL)),
        'mem_norm': gain((D_MODEL,)),
        'w_mem_kv': dense((D_MODEL, 2 * MEM_WIDTH)),
        'mem_q_gain': gain((DEPTH, HEAD_DIM)),
        'mem_k_gain': gain((DEPTH, HEAD_DIM)),
        'fox_w_in': dense((n_fox, D_MODEL, FOX_IN)),
        'fox_b_f': jax.random.uniform(next(ks), (n_fox, N_MIX_HEADS), f32, 1.0, 5.0),
        'fox_q_gain': gain((n_fox, HEAD_DIM)),
        'fox_k_gain': gain((n_fox, HEAD_DIM)),
        'mla_w_in': dense((n_mla, D_MODEL, MLA_IN)),
        'mla_qa_norm': gain((n_mla, MLA_Q_RANK)),
        'mla_kva_norm': gain((n_mla, MLA_KV_RANK)),
        'mla_w_q_up': dense((n_mla, MLA_Q_RANK, N_MIX_HEADS * MLA_QK_DIM)),
        'mla_w_kv_up': dense((n_mla, MLA_KV_RANK, N_MIX_HEADS * (MLA_NOPE_DIM + MLA_V_DIM))),
        'mla_q_gain': gain((n_mla, MLA_QK_DIM)),
        'mla_k_gain': gain((n_mla, MLA_QK_DIM)),
        'moba_w_in': dense((n_moba, D_MODEL, MOBA_IN)),
        'moba_q_gain': gain((n_moba, HEAD_DIM)),
        'moba_k_gain': gain((n_moba, HEAD_DIM)),
        'ffn_w_up': dense((DEPTH, D_MODEL, 2 * D_FF)),
        'ffn_conv_w': dense((DEPTH, CONV_WIDTH, 2 * D_FF)),
        'ffn_conv_b': 0.01 * jax.random.normal(next(ks), (DEPTH, 2 * D_FF), f32),
        'ffn_w_down': dense((DEPTH, D_FF, D_MODEL)),
    }


def reference(x, mem, positions, norm_mix, norm_ffn, w_out, mem_norm, w_mem_kv, mem_q_gain,
              mem_k_gain, fox_w_in, fox_b_f, fox_q_gain, fox_k_gain, mla_w_in, mla_qa_norm,
              mla_kva_norm, mla_w_q_up, mla_w_kv_up, mla_q_gain, mla_k_gain, moba_w_in,
              moba_q_gain, moba_k_gain, ffn_w_up, ffn_conv_w, ffn_conv_b, ffn_w_down):
    b, m = mem.shape[0], mem.shape[1]
    mem_kv = rmsnorm(mem, mem_norm) @ w_mem_kv
    k_mem = mem_kv[..., :MEM_WIDTH].reshape(b, m, N_MEM_HEADS, HEAD_DIM)
    v_mem = mem_kv[..., MEM_WIDTH:].reshape(b, m, N_MEM_HEADS, HEAD_DIM)
    inv_freq = ROPE_THETA ** (-jnp.arange(0, MLA_ROPE_DIM, 2, dtype=jnp.float32) / MLA_ROPE_DIM)
    ang = positions.astype(jnp.float32)[..., None] * inv_freq
    cos = jnp.cos(ang)[:, :, None, :].astype(x.dtype)
    sin = jnp.sin(ang)[:, :, None, :].astype(x.dtype)

    for i in range(DEPTH):
        kind, j = i % N_MIXERS, i // N_MIXERS
        h = rmsnorm(x, norm_mix[i])
        if kind == 0:
            p = h @ fox_w_in[j]
            o_mix = fox_mixer(p[..., :-MEM_WIDTH], fox_b_f[j], fox_q_gain[j], fox_k_gain[j])
        elif kind == 1:
            p = h @ mla_w_in[j]
            o_mix = mla_mixer(p[..., :-MEM_WIDTH], cos, sin, mla_qa_norm[j], mla_kva_norm[j],
                              mla_w_q_up[j], mla_w_kv_up[j], mla_q_gain[j], mla_k_gain[j])
        else:
            p = h @ moba_w_in[j]
            o_mix = moba_mixer(p[..., :-MEM_WIDTH], positions, moba_q_gain[j], moba_k_gain[j])
        o_mem = memory_cross(p[..., -MEM_WIDTH:], k_mem, v_mem, mem_q_gain[i], mem_k_gain[i])
        x = x + jnp.concatenate([o_mix, o_mem], axis=-1) @ w_out[i]
        x = x + conv_ffn(rmsnorm(x, norm_ffn[i]), ffn_w_up[i], ffn_conv_w[i], ffn_conv_b[i],
                         ffn_w_down[i])
    return x
```

```python
import functools

import jax
import jax.numpy as jnp
from jax import lax
from jax.experimental import pallas as pl
from jax.experimental.pallas import tpu as pltpu

F32 = jnp.float32
BF16 = jnp.bfloat16

D_MODEL = 1024
HEAD_DIM = 64
N_MIX_HEADS = 12
MIX_WIDTH = N_MIX_HEADS * HEAD_DIM
N_MEM_HEADS = 4
MEM_WIDTH = N_MEM_HEADS * HEAD_DIM
N_MIXERS = 3
MLA_Q_RANK = 256
MLA_KV_RANK = 128
MLA_NOPE_DIM = 64
MLA_ROPE_DIM = 32
MLA_QK_DIM = MLA_NOPE_DIM + MLA_ROPE_DIM
MLA_V_DIM = 64
ROPE_THETA = 10000.0
MOBA_BLOCK = 256
MOBA_TOPK = 3
D_FF = 2816
EPS = 1e-6

LANES = 128
KV_BLOCK = 256
Q_TILE = 256
ROW_TILE = 512
FF_CHUNK = 256
HALO = 8
MASKED = -1e30
VMEM_LIMIT = 56 * 1024 * 1024


def _params(semantics):
    return pltpu.CompilerParams(dimension_semantics=semantics, vmem_limit_bytes=VMEM_LIMIT)


def _nt_dot(a, b):
    return lax.dot_general(a, b, (((1,), (1,)), ((), ())), preferred_element_type=F32)


def _dot(a, b):
    return jnp.dot(a, b, preferred_element_type=F32)


def _rms(x):
    return x * lax.rsqrt(jnp.mean(x * x, axis=-1, keepdims=True) + EPS)


def _pair_norm(y, gain):
    lane = lax.broadcasted_iota(jnp.int32, y.shape, 1)
    lo = lane < HEAD_DIM
    y2 = y * y
    s_lo = jnp.sum(jnp.where(lo, y2, 0.0), axis=-1, keepdims=True)
    s_hi = jnp.sum(jnp.where(lo, 0.0, y2), axis=-1, keepdims=True)
    ms = jnp.where(lo, s_lo, s_hi) * (1.0 / HEAD_DIM)
    return y * lax.rsqrt(ms + EPS) * gain


def _slab_norm(y, gain, width):
    ms = jnp.sum(y * y, axis=-1, keepdims=True) * (1.0 / width)
    return y * lax.rsqrt(ms + EPS) * gain


def _store_vt(vt_ref, vt, tm):
    for j in range(tm // KV_BLOCK):
        vt_ref[0, j] = vt[:, j * KV_BLOCK:(j + 1) * KV_BLOCK]


def _mem_kernel(mem_ref, g_ref, wk_ref, wvt_ref, kg_ref, kn_ref, vt_ref, *, depth):
    h = (_rms(mem_ref[0]) * g_ref[...]).astype(BF16)
    k = _dot(h, wk_ref[...])
    for s in range(MEM_WIDTH // LANES):
        slab = k[:, s * LANES:(s + 1) * LANES]
        for i in range(depth):
            kn_ref[i, 0, :, s * LANES:(s + 1) * LANES] = _pair_norm(slab, kg_ref[i]).astype(BF16)
    vt_ref[0, 0] = _nt_dot(wvt_ref[...], h).astype(BF16)


def _mem_prep(mem, mem_norm, w_mem_kv, mem_k_gain):
    b, m, d = mem.shape
    depth = mem_k_gain.shape[0]
    wk = w_mem_kv[:, :MEM_WIDTH].astype(BF16)
    wvt = w_mem_kv[:, MEM_WIDTH:].T.astype(BF16)
    kg = jnp.tile(mem_k_gain, (1, 2)).reshape(depth, 1, LANES)
    const = lambda *shape: pl.BlockSpec(shape, lambda i: (0,) * len(shape))
    return pl.pallas_call(
        functools.partial(_mem_kernel, depth=depth),
        grid=(b,),
        in_specs=[pl.BlockSpec((1, m, d), lambda i: (i, 0, 0)), const(1, d), const(d, MEM_WIDTH),
                  const(MEM_WIDTH, d), const(depth, 1, LANES)],
        out_specs=[pl.BlockSpec((depth, 1, m, MEM_WIDTH), lambda i: (0, i, 0, 0)),
                   pl.BlockSpec((1, 1, MEM_WIDTH, m), lambda i: (i, 0, 0, 0))],
        out_shape=[jax.ShapeDtypeStruct((depth, b, m, MEM_WIDTH), BF16),
                   jax.ShapeDtypeStruct((b, 1, MEM_WIDTH, m), BF16)],
        compiler_params=_params(("parallel",)),
        name="mem_prep",
    )(mem, mem_norm.reshape(1, d), wk, wvt, kg)


def _qkv_proj_kernel(x_ref, g_ref, wqk_ref, wvt_ref, wm_ref, qg_ref, kg_ref, mg_ref, *rest,
                     has_gate, tm):
    if has_gate:
        wf_ref, bf_ref, q_ref, k_ref, vt_ref, qm_ref, f_ref = rest
    else:
        q_ref, k_ref, vt_ref, qm_ref = rest
    h = (_rms(x_ref[...]) * g_ref[...]).astype(BF16)
    for c in range(0, 2 * MIX_WIDTH, 2 * LANES):
        y = _dot(h, wqk_ref[:, c:c + 2 * LANES])
        for s in range(2):
            col = c + s * LANES
            slab = y[:, s * LANES:(s + 1) * LANES]
            if col < MIX_WIDTH:
                q_ref[:, col:col + LANES] = (
                    _pair_norm(slab, qg_ref[...]) * HEAD_DIM ** -0.5).astype(BF16)
            else:
                k_ref[:, col - MIX_WIDTH:col - MIX_WIDTH + LANES] = (
                    _pair_norm(slab, kg_ref[...]).astype(BF16))
    _store_vt(vt_ref, _nt_dot(wvt_ref[...], h).astype(BF16), tm)
    ym = _dot(h, wm_ref[...])
    for s in range(MEM_WIDTH // LANES):
        qm_ref[:, s * LANES:(s + 1) * LANES] = (
            _pair_norm(ym[:, s * LANES:(s + 1) * LANES], mg_ref[...]) * HEAD_DIM ** -0.5
        ).astype(BF16)
    if has_gate:
        f = _dot(h, wf_ref[...]) + bf_ref[...]
        f_ref[...] = jnp.minimum(f, 0.0) - jnp.log1p(jnp.exp(-jnp.abs(f)))


def _qkv_proj(x2, seq, g, w_in, q_gain, k_gain, m_gain, b_f=None):
    t, d = x2.shape
    tm = ROW_TILE
    nblk = seq // KV_BLOCK
    per_seq = seq // tm
    has_gate = b_f is not None
    wqk = w_in[:, :2 * MIX_WIDTH].astype(BF16)
    wvt = w_in[:, 2 * MIX_WIDTH:3 * MIX_WIDTH].T.astype(BF16)
    wm = w_in[:, -MEM_WIDTH:].astype(BF16)
    pair = lambda v: jnp.tile(v, 2).reshape(1, LANES)
    const = lambda *shape: pl.BlockSpec(shape, lambda i: (0,) * len(shape))
    rows = lambda w: pl.BlockSpec((tm, w), lambda i: (i, 0))
    args = [x2, g.reshape(1, d), wqk, wvt, wm, pair(q_gain), pair(k_gain), pair(m_gain)]
    in_specs = [rows(d), const(1, d), const(d, 2 * MIX_WIDTH), const(MIX_WIDTH, d),
                const(d, MEM_WIDTH), const(1, LANES), const(1, LANES), const(1, LANES)]
    out_specs = [rows(MIX_WIDTH), rows(MIX_WIDTH),
                 pl.BlockSpec((1, tm // KV_BLOCK, MIX_WIDTH, KV_BLOCK),
                              lambda i: (i // per_seq, i % per_seq, 0, 0)),
                 rows(MEM_WIDTH)]
    out_shape = [jax.ShapeDtypeStruct((t, MIX_WIDTH), BF16), jax.ShapeDtypeStruct((t, MIX_WIDTH), BF16),
                 jax.ShapeDtypeStruct((t // seq, nblk, MIX_WIDTH, KV_BLOCK), BF16),
                 jax.ShapeDtypeStruct((t, MEM_WIDTH), BF16)]
    if has_gate:
        nh = N_MIX_HEADS
        wf = jnp.pad(w_in[:, 3 * MIX_WIDTH:3 * MIX_WIDTH + nh], ((0, 0), (0, LANES - nh))).astype(BF16)
        args += [wf, jnp.pad(b_f, (0, LANES - nh)).reshape(1, LANES)]
        in_specs += [const(d, LANES), const(1, LANES)]
        out_specs.append(rows(LANES))
        out_shape.append(jax.ShapeDtypeStruct((t, LANES), F32))
    return pl.pallas_call(
        functools.partial(_qkv_proj_kernel, has_gate=has_gate, tm=tm),
        grid=(t // tm,), in_specs=in_specs, out_specs=out_specs, out_shape=out_shape,
        compiler_params=_params(("parallel",)),
        name="qkv_proj_gate" if has_gate else "qkv_proj",
    )(*args)


def _cumsum_kernel(f_ref, o_ref, *, seq):
    r = lax.broadcasted_iota(jnp.int32, (KV_BLOCK, KV_BLOCK), 0)
    c = lax.broadcasted_iota(jnp.int32, (KV_BLOCK, KV_BLOCK), 1)
    tri = jnp.where(c <= r, 1.0, 0.0).astype(BF16)
    carry = jnp.zeros((1, LANES), F32)
    for j in range(seq // KV_BLOCK):
        x = f_ref[0, j * KV_BLOCK:(j + 1) * KV_BLOCK, :]
        x1 = x.astype(BF16)
        r1 = x - x1.astype(F32)
        x2 = r1.astype(BF16)
        x3 = (r1 - x2.astype(F32)).astype(BF16)
        cs = _dot(tri, x1) + _dot(tri, x2) + _dot(tri, x3) + carry
        o_ref[0, j * KV_BLOCK:(j + 1) * KV_BLOCK, :] = cs
        carry = cs[KV_BLOCK - 1:KV_BLOCK, :]


def _cumsum(flog):
    b, seq, w = flog.shape
    spec = pl.BlockSpec((1, seq, w), lambda i: (i, 0, 0))
    return pl.pallas_call(
        functools.partial(_cumsum_kernel, seq=seq),
        grid=(b,), in_specs=[spec], out_specs=spec,
        out_shape=jax.ShapeDtypeStruct((b, seq, w), F32),
        compiler_params=_params(("parallel",)),
        name="gate_cumsum",
    )(flog)


def _attn_kernel(*refs, mode, tq, seq, n_sel):
    if mode == "fox":
        q_ref, k_ref, vt_ref, cq_ref, ck_ref, o_ref, m_sc, l_sc, acc_sc, col_sc = refs
    elif mode == "moba":
        (q_ref, k_ref, vt_ref, pq_ref, pk_ref, sl_ref, o_ref,
         m_sc, l_sc, acc_sc, col_sc, km_sc, sel_sc) = refs
    else:
        q_ref, k_ref, vt_ref, o_ref, m_sc, l_sc, acc_sc = refs
    qi = pl.program_id(2)
    tk = KV_BLOCK
    nblk = seq // tk

    if mode == "fox":
        @pl.when(qi == 0)
        def _():
            ck = ck_ref[0, 0]
            for h in range(2):
                col_sc[h] = jnp.broadcast_to(-ck[:, h:h + 1], (seq, LANES))
    if mode == "moba":
        pos0 = pk_ref[0, 0:1, :]
        @pl.when(qi == 0)
        def _():
            pk = (pk_ref[0] - pos0).astype(F32)
            for h in range(2):
                col_sc[h] = pk * sl_ref[0, h:h + 1, :]
            nbp = km_sc.shape[1]
            blk = lax.broadcasted_iota(jnp.int32, (nbp, seq), 0)
            key = lax.broadcasted_iota(jnp.int32, (nbp, seq), 1)
            pool = jnp.where((key >= blk * tk) & (key < (blk + 1) * tk), 1.0, 0.0).astype(BF16)
            km = _dot(pool, k_ref[0]) * (1.0 / tk)
            hi = km.astype(BF16)
            km_sc[0] = hi
            km_sc[1] = (km - hi.astype(F32)).astype(BF16)

    qb = q_ref[0]
    if mode == "mla":
        qh = [qb[:, :LANES], qb[:, LANES:]]
    else:
        qf = qb.astype(F32)
        lane = lax.broadcasted_iota(jnp.int32, qf.shape, 1)
        qh = [jnp.where(lane < HEAD_DIM, qf, 0.0).astype(BF16),
              jnp.where(lane >= HEAD_DIM, qf, 0.0).astype(BF16)]

    if mode == "fox":
        row = [cq_ref[0, 0, h:h + 1, :] for h in range(2)]
    if mode == "moba":
        pq = (pq_ref[0] - pos0).astype(F32)
        row = [-(sl_ref[0, h:h + 1, 0:1] * pq) for h in range(2)]
        if n_sel > 0:
            nbp = km_sc.shape[1]
            n_iota = lax.broadcasted_iota(jnp.int32, (nbp, tq), 0)
            for h in range(2):
                g = _nt_dot(km_sc[0], qh[h]) + _nt_dot(km_sc[1], qh[h])
                g = jnp.where(n_iota < qi, g, -jnp.inf)
                bias = jnp.full((nbp, tq), MASKED, F32)
                for _ in range(n_sel):
                    mx = jnp.max(g, axis=0, keepdims=True)
                    idx = jnp.min(jnp.where(g == mx, n_iota, nbp), axis=0, keepdims=True)
                    hit = n_iota == idx
                    ok = (mx > -jnp.inf) & (mx < jnp.inf)
                    bias = jnp.where(hit & ok, 0.0, bias)
                    g = jnp.where(hit, -jnp.inf, g)
                sel_sc[h] = bias

    def logits(j, h, diag):
        start = j * tk if isinstance(j, int) else pl.multiple_of(j * tk, tk)
        if mode == "mla":
            kb = k_ref[0, pl.ds(start, tk), h * LANES:(h + 1) * LANES]
        else:
            kb = k_ref[0, pl.ds(start, tk), :]
        z = _nt_dot(kb, qh[h])
        if mode in ("fox", "moba"):
            col = col_sc[h, pl.ds(start, tk), :]
            z = (z + row[h]) + jnp.concatenate([col] * (tq // LANES), axis=1)
        if mode == "moba" and not diag and n_sel > 0:
            z = z + sel_sc[h, pl.ds(j, 1), :]
        if diag and mode != "mem":
            key = lax.broadcasted_iota(jnp.int32, (tk, tq), 0)
            qry = lax.broadcasted_iota(jnp.int32, (tk, tq), 1)
            z = jnp.where(key <= qry, z, MASKED)
        return z

    def values(j, h):
        return vt_ref[0, j, h * HEAD_DIM:(h + 1) * HEAD_DIM, :]

    jd = qi if mode != "mem" else 0
    for h in range(2):
        z = logits(jd, h, True)
        m = jnp.max(z, axis=0, keepdims=True)
        p = jnp.exp(z - m)
        m_sc[h] = m
        l_sc[h] = jnp.sum(p, axis=0, keepdims=True)
        acc_sc[h] = _dot(values(jd, h), p.astype(BF16))

    if mode != "mem":
        def body(j, carry):
            for h in range(2):
                z = logits(j, h, False)
                m_prev = m_sc[h]
                m_new = jnp.maximum(m_prev, jnp.max(z, axis=0, keepdims=True))
                alpha = jnp.exp(m_prev - m_new)
                p = jnp.exp(z - m_new)
                l_sc[h] = alpha * l_sc[h] + jnp.sum(p, axis=0, keepdims=True)
                acc_sc[h] = alpha * acc_sc[h] + _dot(values(j, h), p.astype(BF16))
                m_sc[h] = m_new
            return carry
        lax.fori_loop(0, qi, body, 0)

    out_t = jnp.concatenate([acc_sc[h] / l_sc[h] for h in range(2)], axis=0)
    o_ref[0] = out_t.T.astype(BF16)


def _attention(mode, q, k, vt, extra=(), n_sel=0):
    b, s, _ = q.shape
    sk = k.shape[1]
    n_pairs = vt.shape[2] // LANES
    tq = Q_TILE
    qw = 2 * LANES if mode == "mla" else LANES
    in_specs = [pl.BlockSpec((1, tq, qw), lambda bi, p, i: (bi, i, p)),
                pl.BlockSpec((1, sk, qw), lambda bi, p, i: (bi, 0, p)),
                pl.BlockSpec((1, sk // KV_BLOCK, LANES, KV_BLOCK), lambda bi, p, i: (bi, 0, p, 0))]
    scratch = [pltpu.VMEM((2, 1, tq), F32), pltpu.VMEM((2, 1, tq), F32),
               pltpu.VMEM((2, HEAD_DIM, tq), F32)]
    if mode == "fox":
        in_specs += [pl.BlockSpec((1, 1, 2, tq), lambda bi, p, i: (bi, p, 0, i)),
                     pl.BlockSpec((1, 1, s, 2), lambda bi, p, i: (bi, p, 0, 0))]
        scratch += [pltpu.VMEM((2, s, LANES), F32)]
    elif mode == "moba":
        nbp = -(-(s // KV_BLOCK) // 16) * 16
        in_specs += [pl.BlockSpec((1, 1, tq), lambda bi, p, i: (bi, 0, i)),
                     pl.BlockSpec((1, s, 1), lambda bi, p, i: (bi, 0, 0)),
                     pl.BlockSpec((1, 2, LANES), lambda bi, p, i: (p, 0, 0))]
        scratch += [pltpu.VMEM((2, s, LANES), F32), pltpu.VMEM((2, nbp, LANES), BF16),
                    pltpu.VMEM((2, nbp, tq), F32)]
    return pl.pallas_call(
        functools.partial(_attn_kernel, mode=mode, tq=tq, seq=sk, n_sel=n_sel),
        grid=(b, n_pairs, s // tq),
        in_specs=in_specs,
        out_specs=pl.BlockSpec((1, tq, LANES), lambda bi, p, i: (bi, i, p)),
        out_shape=jax.ShapeDtypeStruct((b, s, n_pairs * LANES), BF16),
        scratch_shapes=scratch,
        compiler_params=_params(("parallel", "parallel", "arbitrary")),
        name="attn_" + mode,
    )(q, k, vt, *extra)


def _mla_proj_kernel(x_ref, pos_ref, g_ref, w1_ref, qa_ref, kva_ref, wq_ref, wqr_ref, wk_ref,
                     wvt_ref, invf_ref, qg_ref, kg_ref, mg_ref, q_ref, k_ref, vt_ref, qm_ref, *, tm):
    h = (_rms(x_ref[...]) * g_ref[...]).astype(BF16)
    p = _dot(h, w1_ref[...])
    o = MLA_Q_RANK
    q_lat = p[:, :o]
    kv_lat = p[:, o:o + LANES]
    kr = p[:, o + LANES:o + 2 * LANES]
    kr_rot = p[:, o + 2 * LANES:o + 3 * LANES]
    ym = p[:, o + 3 * LANES:]
    qn = (_rms(q_lat) * qa_ref[...]).astype(BF16)
    kvn = (_rms(kv_lat) * kva_ref[...]).astype(BF16)
    ang = pos_ref[...].astype(F32) * invf_ref[...]
    cos, sin = jnp.cos(ang), jnp.sin(ang)
    k_rope = kr * cos + kr_rot * sin
    width = N_MIX_HEADS * LANES
    for c in range(0, width, 2 * LANES):
        qc = _dot(qn, wq_ref[:, c:c + 2 * LANES])
        qr = _dot(qn, wqr_ref[:, c:c + 2 * LANES])
        kc = _dot(kvn, wk_ref[:, c:c + 2 * LANES])
        for s in range(2):
            sl = slice(s * LANES, (s + 1) * LANES)
            out = slice(c + s * LANES, c + (s + 1) * LANES)
            qq = qc[:, sl] * cos + qr[:, sl] * sin
            q_ref[:, out] = (_slab_norm(qq, qg_ref[...], MLA_QK_DIM) * MLA_QK_DIM ** -0.5).astype(BF16)
            k_ref[:, out] = _slab_norm(kc[:, sl] + k_rope, kg_ref[...], MLA_QK_DIM).astype(BF16)
    _store_vt(vt_ref, _nt_dot(wvt_ref[...], kvn).astype(BF16), tm)
    for s in range(MEM_WIDTH // LANES):
        qm_ref[:, s * LANES:(s + 1) * LANES] = (
            _pair_norm(ym[:, s * LANES:(s + 1) * LANES], mg_ref[...]) * HEAD_DIM ** -0.5
        ).astype(BF16)


def _rot_half(w):
    half = MLA_ROPE_DIM // 2
    return jnp.concatenate([-w[..., half:], w[..., :half]], axis=-1)


def _mla_proj(x2, pos2, seq, g, w_in, qa, kva, w_q_up, w_kv_up, q_gain, k_gain, m_gain):
    t, d = x2.shape
    tm = ROW_TILE
    per_seq = seq // tm
    nh, nope, rope_d = N_MIX_HEADS, MLA_NOPE_DIM, MLA_ROPE_DIM
    pad_r = LANES - nope - rope_d
    o = MLA_Q_RANK + MLA_KV_RANK
    k_r = w_in[:, o:o + rope_d]
    slab = lambda w: jnp.pad(w, ((0, 0), (nope, pad_r)))
    w1 = jnp.concatenate([w_in[:, :o], slab(k_r), slab(_rot_half(k_r)), w_in[:, -MEM_WIDTH:]],
                         axis=1).astype(BF16)
    wq3 = w_q_up.reshape(MLA_Q_RANK, nh, MLA_QK_DIM)
    wq = jnp.pad(wq3, ((0, 0), (0, 0), (0, pad_r))).reshape(MLA_Q_RANK, nh * LANES).astype(BF16)
    wqr = jnp.pad(_rot_half(wq3[..., nope:]), ((0, 0), (0, 0), (nope, pad_r))
                  ).reshape(MLA_Q_RANK, nh * LANES).astype(BF16)
    wkv3 = w_kv_up.reshape(MLA_KV_RANK, nh, nope + MLA_V_DIM)
    wk = jnp.pad(wkv3[..., :nope], ((0, 0), (0, 0), (0, LANES - nope))
                 ).reshape(MLA_KV_RANK, nh * LANES).astype(BF16)
    wvt = wkv3[..., nope:].reshape(MLA_KV_RANK, nh * MLA_V_DIM).T.astype(BF16)
    inv_freq = ROPE_THETA ** (-jnp.arange(0, rope_d, 2, dtype=F32) / rope_d)
    invf = jnp.pad(jnp.tile(inv_freq, 2), (nope, pad_r)).reshape(1, LANES)
    padg = lambda v: jnp.pad(v, (0, pad_r)).reshape(1, LANES)
    const = lambda *shape: pl.BlockSpec(shape, lambda i: (0,) * len(shape))
    rows = lambda w: pl.BlockSpec((tm, w), lambda i: (i, 0))
    qkw = nh * LANES
    return pl.pallas_call(
        functools.partial(_mla_proj_kernel, tm=tm),
        grid=(t // tm,),
        in_specs=[rows(d), rows(1), const(1, d), const(d, w1.shape[1]), const(1, MLA_Q_RANK),
                  const(1, MLA_KV_RANK), const(MLA_Q_RANK, qkw), const(MLA_Q_RANK, qkw),
                  const(MLA_KV_RANK, qkw), const(MIX_WIDTH, MLA_KV_RANK), const(1, LANES),
                  const(1, LANES), const(1, LANES), const(1, LANES)],
        out_specs=[rows(qkw), rows(qkw),
                   pl.BlockSpec((1, tm // KV_BLOCK, MIX_WIDTH, KV_BLOCK),
                                lambda i: (i // per_seq, i % per_seq, 0, 0)),
                   rows(MEM_WIDTH)],
        out_shape=[jax.ShapeDtypeStruct((t, qkw), BF16), jax.ShapeDtypeStruct((t, qkw), BF16),
                   jax.ShapeDtypeStruct((t // seq, seq // KV_BLOCK, MIX_WIDTH, KV_BLOCK), BF16),
                   jax.ShapeDtypeStruct((t, MEM_WIDTH), BF16)],
        compiler_params=_params(("parallel",)),
        name="mla_proj",
    )(x2, pos2, g.reshape(1, d), w1, qa.reshape(1, -1), kva.reshape(1, -1), wq, wqr, wk, wvt, invf,
      padg(q_gain), padg(k_gain), jnp.tile(m_gain, 2).reshape(1, LANES))


def _out_proj_kernel(om_ref, oc_ref, x_ref, w1_ref, w2_ref, o_ref):
    o_ref[...] = x_ref[...] + _dot(om_ref[...], w1_ref[...]) + _dot(oc_ref[...], w2_ref[...])


def _out_proj(o_mix, o_mem, x2, w_out):
    t, d = x2.shape
    tm = ROW_TILE
    rows = lambda w: pl.BlockSpec((tm, w), lambda i: (i, 0))
    const = lambda *shape: pl.BlockSpec(shape, lambda i: (0,) * len(shape))
    return pl.pallas_call(
        _out_proj_kernel, grid=(t // tm,),
        in_specs=[rows(MIX_WIDTH), rows(MEM_WIDTH), rows(d), const(MIX_WIDTH, d), const(MEM_WIDTH, d)],
        out_specs=rows(d), out_shape=jax.ShapeDtypeStruct((t, d), F32),
        compiler_params=_params(("parallel",)),
        name="out_proj",
    )(o_mix, o_mem, x2, w_out[:MIX_WIDTH].astype(BF16), w_out[MIX_WIDTH:].astype(BF16))


def _ffn_kernel(x_ref, halo_ref, g_ref, wup_ref, cw_ref, cb_ref, wdn_ref, o_ref, u_sc, *, tm, per_seq):
    i = pl.program_id(0)
    x = x_ref[...]
    h = (_rms(x) * g_ref[...]).astype(BF16)
    hh = (_rms(halo_ref[...]) * g_ref[...]).astype(BF16)
    keep = jnp.where(i % per_seq == 0, 0.0, 1.0)
    tf = FF_CHUNK
    acc = x
    for c in range(D_FF // tf):
        halves = []
        for part in range(2):
            cols = slice(part * D_FF + c * tf, part * D_FF + (c + 1) * tf)
            w = wup_ref[:, cols]
            u_sc[0:HALO, :] = _dot(hh, w) * keep
            u = _dot(h, w)
            u_sc[HALO:HALO + tm, :] = u
            cw = cw_ref[:, cols]
            halves.append(cw[2:3] * u + cw[1:2] * u_sc[HALO - 1:HALO - 1 + tm, :]
                          + cw[0:1] * u_sc[HALO - 2:HALO - 2 + tm, :] + cb_ref[:, cols])
        gate, val = halves
        act = (gate * jax.nn.sigmoid(gate) * val).astype(BF16)
        acc = acc + _dot(act, wdn_ref[c * tf:(c + 1) * tf, :])
    o_ref[...] = acc


def _ffn(x2, seq, g, w_up, conv_w, conv_b, w_down):
    t, d = x2.shape
    tm = ROW_TILE
    per_seq = seq // tm
    rows = pl.BlockSpec((tm, d), lambda i: (i, 0))
    halo = pl.BlockSpec((HALO, d), lambda i: (jnp.maximum(i * (tm // HALO) - 1, 0), 0))
    resident = lambda *shape: pl.BlockSpec(shape, lambda i: (0,) * len(shape),
                                           pipeline_mode=pl.Buffered(1))
    return pl.pallas_call(
        functools.partial(_ffn_kernel, tm=tm, per_seq=per_seq),
        grid=(t // tm,),
        in_specs=[rows, halo, resident(1, d), resident(d, 2 * D_FF), resident(3, 2 * D_FF),
                  resident(1, 2 * D_FF), resident(D_FF, d)],
        out_specs=rows, out_shape=jax.ShapeDtypeStruct((t, d), F32),
        scratch_shapes=[pltpu.VMEM((HALO + tm, FF_CHUNK), F32)],
        compiler_params=_params(("parallel",)),
        name="conv_ffn",
    )(x2, x2, g.reshape(1, d), w_up.astype(BF16), conv_w, conv_b.reshape(1, -1), w_down.astype(BF16))


def kernel(x, mem, positions, norm_mix, norm_ffn, w_out, mem_norm, w_mem_kv, mem_q_gain,
           mem_k_gain, fox_w_in, fox_b_f, fox_q_gain, fox_k_gain, mla_w_in, mla_qa_norm,
           mla_kva_norm, mla_w_q_up, mla_w_kv_up, mla_q_gain, mla_k_gain, moba_w_in,
           moba_q_gain, moba_k_gain, ffn_w_up, ffn_conv_w, ffn_conv_b, ffn_w_down):
    b, s, d = x.shape
    depth = norm_mix.shape[0]
    assert d == D_MODEL and s % ROW_TILE == 0 and mem.shape[1] == KV_BLOCK
    n_pairs = N_MIX_HEADS // 2
    k_mem, vt_mem = _mem_prep(mem, mem_norm, w_mem_kv, mem_k_gain)
    x2 = x.reshape(b * s, d)
    pos2 = positions.reshape(b * s, 1)
    n_blocks = s // MOBA_BLOCK
    slopes = 2.0 ** (-8.0 * jnp.arange(1, N_MIX_HEADS + 1, dtype=F32) / N_MIX_HEADS)
    slopes = jnp.broadcast_to(slopes.reshape(n_pairs, 2, 1), (n_pairs, 2, LANES))

    for i in range(depth):
        kind, j = i % N_MIXERS, i // N_MIXERS
        if kind == 0:
            q, k, vt, qm, flog = _qkv_proj(x2, s, norm_mix[i], fox_w_in[j], fox_q_gain[j],
                                           fox_k_gain[j], mem_q_gain[i], fox_b_f[j])
            cum = _cumsum(flog.reshape(b, s, LANES))[..., :N_MIX_HEADS]
            cq = cum.transpose(0, 2, 1).reshape(b, n_pairs, 2, s)
            ck = cum.reshape(b, s, n_pairs, 2).transpose(0, 2, 1, 3)
            o_mix = _attention("fox", q.reshape(b, s, -1), k.reshape(b, s, -1), vt, (cq, ck))
        elif kind == 1:
            q, k, vt, qm = _mla_proj(x2, pos2, s, norm_mix[i], mla_w_in[j], mla_qa_norm[j],
                                     mla_kva_norm[j], mla_w_q_up[j], mla_w_kv_up[j],
                                     mla_q_gain[j], mla_k_gain[j], mem_q_gain[i])
            o_mix = _attention("mla", q.reshape(b, s, -1), k.reshape(b, s, -1), vt)
        else:
            q, k, vt, qm = _qkv_proj(x2, s, norm_mix[i], moba_w_in[j], moba_q_gain[j],
                                     moba_k_gain[j], mem_q_gain[i])
            extra = (positions.reshape(b, 1, s), positions.reshape(b, s, 1), slopes)
            o_mix = _attention("moba", q.reshape(b, s, -1), k.reshape(b, s, -1), vt, extra,
                               n_sel=min(MOBA_TOPK, n_blocks - 1))
        o_mem = _attention("mem", qm.reshape(b, s, MEM_WIDTH), k_mem[i], vt_mem)
        x2 = _out_proj(o_mix.reshape(b * s, MIX_WIDTH), o_mem.reshape(b * s, MEM_WIDTH), x2, w_out[i])
        x2 = _ffn(x2, s, norm_ffn[i], ffn_w_up[i], ffn_conv_w[i], ffn_conv_b[i], ffn_w_down[i])
    return x2.reshape(b, s, d)
```

```python
import functools

import jax
import jax.numpy as jnp
from jax import lax
from jax.experimental import pallas as pl
from jax.experimental.pallas import tpu as pltpu

F32 = jnp.float32
BF16 = jnp.bfloat16

D_MODEL = 1024
HEAD_DIM = 64
N_MIX_HEADS = 12
MIX_WIDTH = N_MIX_HEADS * HEAD_DIM
N_MEM_HEADS = 4
MEM_WIDTH = N_MEM_HEADS * HEAD_DIM
N_MIXERS = 3
MLA_Q_RANK = 256
MLA_KV_RANK = 128
MLA_NOPE_DIM = 64
MLA_ROPE_DIM = 32
MLA_QK_DIM = MLA_NOPE_DIM + MLA_ROPE_DIM
MLA_V_DIM = 64
ROPE_THETA = 10000.0
MOBA_BLOCK = 256
MOBA_TOPK = 3
D_FF = 2816
EPS = 1e-6

LANES = 128
KV_BLOCK = 256
Q_TILE = 256
ATTN_SUB = 4
ATTN_LOOKAHEAD = 5
ROW_TILE = 512
FF_CHUNK = 256
HALO = 8
MASKED = -1e30
LOG2E = 1.4426950408889634
DENOM_ROWS = 16
VMEM_LIMIT = 56 * 1024 * 1024


def _params(semantics):
    return pltpu.CompilerParams(dimension_semantics=semantics, vmem_limit_bytes=VMEM_LIMIT)


def _nt_dot(a, b):
    return lax.dot_general(a, b, (((1,), (1,)), ((), ())), preferred_element_type=F32)


def _dot(a, b):
    return jnp.dot(a, b, preferred_element_type=F32)


def _rms(x):
    return x * lax.rsqrt(jnp.mean(x * x, axis=-1, keepdims=True) + EPS)


def _pair_norm(y, gain):
    lane = lax.broadcasted_iota(jnp.int32, y.shape, 1)
    lo = lane < HEAD_DIM
    y2 = y * y
    s_lo = jnp.sum(jnp.where(lo, y2, 0.0), axis=-1, keepdims=True)
    s_hi = jnp.sum(jnp.where(lo, 0.0, y2), axis=-1, keepdims=True)
    ms = jnp.where(lo, s_lo, s_hi) * (1.0 / HEAD_DIM)
    return y * lax.rsqrt(ms + EPS) * gain


def _slab_norm(y, gain, width):
    ms = jnp.sum(y * y, axis=-1, keepdims=True) * (1.0 / width)
    return y * lax.rsqrt(ms + EPS) * gain


def _store_vt(vt_ref, vt, tm):
    for j in range(tm // KV_BLOCK):
        vt_ref[0, j] = vt[:, j * KV_BLOCK:(j + 1) * KV_BLOCK]


def _mem_kernel(mem_ref, g_ref, wk_ref, wvt_ref, kg_ref, kn_ref, vt_ref, *, depth):
    h = (_rms(mem_ref[0]) * g_ref[...]).astype(BF16)
    k = _dot(h, wk_ref[...])
    for s in range(MEM_WIDTH // LANES):
        slab = k[:, s * LANES:(s + 1) * LANES]
        for i in range(depth):
            kn_ref[i, 0, :, s * LANES:(s + 1) * LANES] = _pair_norm(slab, kg_ref[i]).astype(BF16)
    vt_ref[0, 0] = _nt_dot(wvt_ref[...], h).astype(BF16)


def _mem_prep(mem, mem_norm, w_mem_kv, mem_k_gain):
    b, m, d = mem.shape
    depth = mem_k_gain.shape[0]
    wk = w_mem_kv[:, :MEM_WIDTH].astype(BF16)
    wvt = w_mem_kv[:, MEM_WIDTH:].T.astype(BF16)
    kg = jnp.tile(mem_k_gain, (1, 2)).reshape(depth, 1, LANES)
    const = lambda *shape: pl.BlockSpec(shape, lambda i: (0,) * len(shape))
    return pl.pallas_call(
        functools.partial(_mem_kernel, depth=depth),
        grid=(b,),
        in_specs=[pl.BlockSpec((1, m, d), lambda i: (i, 0, 0)), const(1, d), const(d, MEM_WIDTH),
                  const(MEM_WIDTH, d), const(depth, 1, LANES)],
        out_specs=[pl.BlockSpec((depth, 1, m, MEM_WIDTH), lambda i: (0, i, 0, 0)),
                   pl.BlockSpec((1, 1, MEM_WIDTH, m), lambda i: (i, 0, 0, 0))],
        out_shape=[jax.ShapeDtypeStruct((depth, b, m, MEM_WIDTH), BF16),
                   jax.ShapeDtypeStruct((b, 1, MEM_WIDTH, m), BF16)],
        compiler_params=_params(("parallel",)),
        name="mem_prep",
    )(mem, mem_norm.reshape(1, d), wk, wvt, kg)


def _qkv_proj_kernel(x_ref, g_ref, wqk_ref, wvt_ref, wm_ref, qg_ref, kg_ref, mg_ref, *rest,
                     has_gate, tm):
    if has_gate:
        wf_ref, bf_ref, q_ref, k_ref, vt_ref, qm_ref, f_ref = rest
    else:
        q_ref, k_ref, vt_ref, qm_ref = rest
    h = (_rms(x_ref[...]) * g_ref[...]).astype(BF16)
    for c in range(0, 2 * MIX_WIDTH, 2 * LANES):
        y = _dot(h, wqk_ref[:, c:c + 2 * LANES])
        for s in range(2):
            col = c + s * LANES
            slab = y[:, s * LANES:(s + 1) * LANES]
            if col < MIX_WIDTH:
                q_ref[:, col:col + LANES] = (
                    _pair_norm(slab, qg_ref[...]) * (HEAD_DIM ** -0.5 * LOG2E)).astype(BF16)
            else:
                k_ref[:, col - MIX_WIDTH:col - MIX_WIDTH + LANES] = (
                    _pair_norm(slab, kg_ref[...]).astype(BF16))
    _store_vt(vt_ref, _nt_dot(wvt_ref[...], h).astype(BF16), tm)
    ym = _dot(h, wm_ref[...])
    for s in range(MEM_WIDTH // LANES):
        qm_ref[:, s * LANES:(s + 1) * LANES] = (
            _pair_norm(ym[:, s * LANES:(s + 1) * LANES], mg_ref[...]) * (HEAD_DIM ** -0.5 * LOG2E)
        ).astype(BF16)
    if has_gate:
        f = _dot(h, wf_ref[...]) + bf_ref[...]
        f_ref[...] = jnp.minimum(f, 0.0) - jnp.log1p(jnp.exp(-jnp.abs(f)))


def _qkv_proj(x2, seq, g, w_in, q_gain, k_gain, m_gain, b_f=None):
    t, d = x2.shape
    tm = ROW_TILE
    nblk = seq // KV_BLOCK
    per_seq = seq // tm
    has_gate = b_f is not None
    wqk = w_in[:, :2 * MIX_WIDTH].astype(BF16)
    wvt = w_in[:, 2 * MIX_WIDTH:3 * MIX_WIDTH].T.astype(BF16)
    wm = w_in[:, -MEM_WIDTH:].astype(BF16)
    pair = lambda v: jnp.tile(v, 2).reshape(1, LANES)
    const = lambda *shape: pl.BlockSpec(shape, lambda i: (0,) * len(shape))
    rows = lambda w: pl.BlockSpec((tm, w), lambda i: (i, 0))
    args = [x2, g.reshape(1, d), wqk, wvt, wm, pair(q_gain), pair(k_gain), pair(m_gain)]
    in_specs = [rows(d), const(1, d), const(d, 2 * MIX_WIDTH), const(MIX_WIDTH, d),
                const(d, MEM_WIDTH), const(1, LANES), const(1, LANES), const(1, LANES)]
    out_specs = [rows(MIX_WIDTH), rows(MIX_WIDTH),
                 pl.BlockSpec((1, tm // KV_BLOCK, MIX_WIDTH, KV_BLOCK),
                              lambda i: (i // per_seq, i % per_seq, 0, 0)),
                 rows(MEM_WIDTH)]
    out_shape = [jax.ShapeDtypeStruct((t, MIX_WIDTH), BF16), jax.ShapeDtypeStruct((t, MIX_WIDTH), BF16),
                 jax.ShapeDtypeStruct((t // seq, nblk, MIX_WIDTH, KV_BLOCK), BF16),
                 jax.ShapeDtypeStruct((t, MEM_WIDTH), BF16)]
    if has_gate:
        nh = N_MIX_HEADS
        wf = jnp.pad(w_in[:, 3 * MIX_WIDTH:3 * MIX_WIDTH + nh], ((0, 0), (0, LANES - nh))).astype(BF16)
        args += [wf, jnp.pad(b_f, (0, LANES - nh)).reshape(1, LANES)]
        in_specs += [const(d, LANES), const(1, LANES)]
        out_specs.append(rows(LANES))
        out_shape.append(jax.ShapeDtypeStruct((t, LANES), F32))
    return pl.pallas_call(
        functools.partial(_qkv_proj_kernel, has_gate=has_gate, tm=tm),
        grid=(t // tm,), in_specs=in_specs, out_specs=out_specs, out_shape=out_shape,
        compiler_params=_params(("parallel",)),
        name="qkv_proj_gate" if has_gate else "qkv_proj",
    )(*args)


def _cumsum_kernel(f_ref, o_ref, *, seq):
    r = lax.broadcasted_iota(jnp.int32, (KV_BLOCK, KV_BLOCK), 0)
    c = lax.broadcasted_iota(jnp.int32, (KV_BLOCK, KV_BLOCK), 1)
    tri = jnp.where(c <= r, 1.0, 0.0).astype(BF16)
    carry = jnp.zeros((1, LANES), F32)
    for j in range(seq // KV_BLOCK):
        x = f_ref[0, j * KV_BLOCK:(j + 1) * KV_BLOCK, :]
        x1 = x.astype(BF16)
        r1 = x - x1.astype(F32)
        x2 = r1.astype(BF16)
        x3 = (r1 - x2.astype(F32)).astype(BF16)
        cs = _dot(tri, x1) + _dot(tri, x2) + _dot(tri, x3) + carry
        o_ref[0, j * KV_BLOCK:(j + 1) * KV_BLOCK, :] = cs
        carry = cs[KV_BLOCK - 1:KV_BLOCK, :]


def _cumsum(flog):
    b, seq, w = flog.shape
    spec = pl.BlockSpec((1, seq, w), lambda i: (i, 0, 0))
    return pl.pallas_call(
        functools.partial(_cumsum_kernel, seq=seq),
        grid=(b,), in_specs=[spec], out_specs=spec,
        out_shape=jax.ShapeDtypeStruct((b, seq, w), F32),
        compiler_params=_params(("parallel",)),
        name="gate_cumsum",
    )(flog)


def _attn_kernel(*refs, mode, tq, sub, seq, n_sel):
    if mode == "fox":
        q_ref, k_ref, vt_ref, cq_ref, ck_ref, o_ref, m_sc, acc_sc, col_sc = refs
    elif mode == "moba":
        (q_ref, k_ref, vt_ref, pq_ref, pk_ref, sl_ref, o_ref,
         m_sc, acc_sc, col_sc, km_sc, sel_sc) = refs
    else:
        q_ref, k_ref, vt_ref, o_ref, m_sc, acc_sc = refs
    qi = pl.program_id(2)
    tk = KV_BLOCK
    nblk = seq // tk

    if mode == "fox":
        @pl.when(qi == 0)
        def _():
            ck = ck_ref[0, 0]
            for h in range(2):
                col_sc[h] = jnp.broadcast_to(ck[:, h:h + 1] * -LOG2E, (seq, LANES))
    if mode == "moba":
        pos0 = pk_ref[0, 0:1, :]
        @pl.when(qi == 0)
        def _():
            pk = (pk_ref[0] - pos0).astype(F32)
            for h in range(2):
                col_sc[h] = pk * (sl_ref[0, h:h + 1, :] * LOG2E)
            nbp = km_sc.shape[1]
            blk = lax.broadcasted_iota(jnp.int32, (nbp, seq), 0)
            key = lax.broadcasted_iota(jnp.int32, (nbp, seq), 1)
            pool = jnp.where((key >= blk * tk) & (key < (blk + 1) * tk), 1.0, 0.0).astype(BF16)
            km = _dot(pool, k_ref[0]) * (1.0 / tk)
            hi = km.astype(BF16)
            km_sc[0] = hi
            km_sc[1] = (km - hi.astype(F32)).astype(BF16)

    blk0 = qi * sub
    chains = [(i, h) for i in range(sub) for h in range(2)]

    qh, row = {}, {}
    for i in range(sub):
        qb = q_ref[0, i * tq:(i + 1) * tq, :]
        if mode == "mla":
            qh[i, 0], qh[i, 1] = qb[:, :LANES], qb[:, LANES:]
        else:
            qf = qb.astype(F32)
            lane = lax.broadcasted_iota(jnp.int32, qf.shape, 1)
            qh[i, 0] = jnp.where(lane < HEAD_DIM, qf, 0.0).astype(BF16)
            qh[i, 1] = jnp.where(lane >= HEAD_DIM, qf, 0.0).astype(BF16)
        for h in range(2):
            if mode == "fox":
                row[i, h] = cq_ref[0, 0, h:h + 1, i * tq:(i + 1) * tq] * LOG2E
            if mode == "moba":
                pq = (pq_ref[0, :, i * tq:(i + 1) * tq] - pos0).astype(F32)
                row[i, h] = -((sl_ref[0, h:h + 1, 0:1] * LOG2E) * pq)

    if mode == "moba" and n_sel > 0:
        nbp = km_sc.shape[1]
        n_iota = lax.broadcasted_iota(jnp.int32, (nbp, tq), 0)
        for i, h in chains:
            g = _nt_dot(km_sc[0], qh[i, h]) + _nt_dot(km_sc[1], qh[i, h])
            g = jnp.where(n_iota < blk0 + i, g, -jnp.inf)
            bias = jnp.full((nbp, tq), MASKED, F32)
            for _ in range(n_sel):
                mx = jnp.max(g, axis=0, keepdims=True)
                idx = jnp.min(jnp.where(g == mx, n_iota, nbp), axis=0, keepdims=True)
                hit = n_iota == idx
                ok = (mx > -jnp.inf) & (mx < jnp.inf)
                bias = jnp.where(hit & ok, 0.0, bias)
                g = jnp.where(hit, -jnp.inf, g)
            sel_sc[i, h] = bias

    def logits(step):
        i, h, j, own = step
        start = j * tk if isinstance(j, int) else pl.multiple_of(j * tk, tk)
        if mode == "mla":
            kb = k_ref[0, pl.ds(start, tk), h * LANES:(h + 1) * LANES]
        else:
            kb = k_ref[0, pl.ds(start, tk), :]
        z = _nt_dot(kb, qh[i, h])
        if mode in ("fox", "moba"):
            col = col_sc[h, pl.ds(start, tk), :]
            z = (z + row[i, h]) + jnp.concatenate([col] * (tq // LANES), axis=1)
        if own and mode != "mem":
            key = lax.broadcasted_iota(jnp.int32, (tk, tq), 0)
            qry = lax.broadcasted_iota(jnp.int32, (tk, tq), 1)
            z = jnp.where(key <= qry, z, MASKED)
        elif mode == "moba" and n_sel > 0:
            z = z + sel_sc[i, h, pl.ds(j, 1), :]
        return z

    ones_rows = jnp.ones((DENOM_ROWS, tk), BF16)

    def fold(step, z):
        i, h, j, own = step
        vt = jnp.concatenate([vt_ref[0, j, h * HEAD_DIM:(h + 1) * HEAD_DIM, :], ones_rows], axis=0)
        mz = jnp.max(z, axis=0, keepdims=True)
        if own:
            m_sc[i, h] = mz
            acc_sc[i, h] = _dot(vt, jnp.exp2(z - mz).astype(BF16))
        else:
            m_prev = m_sc[i, h]
            m_new = jnp.maximum(m_prev, mz)
            alpha = jnp.exp2(m_prev - m_new)
            acc_sc[i, h] = alpha * acc_sc[i, h] + _dot(vt, jnp.exp2(z - m_new).astype(BF16))
            m_sc[i, h] = m_new

    def run(steps):
        zs = {s: logits(steps[s]) for s in range(min(ATTN_LOOKAHEAD, len(steps)))}
        for s in range(len(steps)):
            if s + ATTN_LOOKAHEAD < len(steps):
                zs[s + ATTN_LOOKAHEAD] = logits(steps[s + ATTN_LOOKAHEAD])
            fold(steps[s], zs.pop(s))

    if mode == "mem":
        run([(i, h, 0, True) for i, h in chains])
    else:
        steps = [(i, h, blk0 + i, True) for i, h in chains]
        steps += [(i, h, blk0 + b, False) for b in range(sub - 1) for i, h in chains if i > b]
        run(steps)

        def body(g, carry):
            run([(i, h, g * sub + b, False) for b in range(sub) for i, h in chains])
            return carry
        lax.fori_loop(0, qi, body, 0)

    for i in range(sub):
        out_t = jnp.concatenate(
            [acc_sc[i, h, :HEAD_DIM, :] / acc_sc[i, h, HEAD_DIM:HEAD_DIM + 1, :] for h in range(2)],
            axis=0)
        o_ref[0, i * tq:(i + 1) * tq, :] = out_t.T.astype(BF16)


def _attention(mode, q, k, vt, extra=(), n_sel=0):
    b, s, _ = q.shape
    sk = k.shape[1]
    n_pairs = vt.shape[2] // LANES
    sub = ATTN_SUB
    tq = Q_TILE
    ts = sub * tq
    assert s % ts == 0
    qw = 2 * LANES if mode == "mla" else LANES
    in_specs = [pl.BlockSpec((1, ts, qw), lambda bi, p, i: (bi, i, p)),
                pl.BlockSpec((1, sk, qw), lambda bi, p, i: (bi, 0, p)),
                pl.BlockSpec((1, sk // KV_BLOCK, LANES, KV_BLOCK), lambda bi, p, i: (bi, 0, p, 0))]
    scratch = [pltpu.VMEM((sub, 2, 1, tq), F32),
               pltpu.VMEM((sub, 2, HEAD_DIM + DENOM_ROWS, tq), F32)]
    if mode == "fox":
        in_specs += [pl.BlockSpec((1, 1, 2, ts), lambda bi, p, i: (bi, p, 0, i)),
                     pl.BlockSpec((1, 1, s, 2), lambda bi, p, i: (bi, p, 0, 0))]
        scratch += [pltpu.VMEM((2, s, LANES), F32)]
    elif mode == "moba":
        nbp = -(-(s // KV_BLOCK) // 16) * 16
        in_specs += [pl.BlockSpec((1, 1, ts), lambda bi, p, i: (bi, 0, i)),
                     pl.BlockSpec((1, s, 1), lambda bi, p, i: (bi, 0, 0)),
                     pl.BlockSpec((1, 2, LANES), lambda bi, p, i: (p, 0, 0))]
        scratch += [pltpu.VMEM((2, s, LANES), F32), pltpu.VMEM((2, nbp, LANES), BF16),
                    pltpu.VMEM((sub, 2, nbp, tq), F32)]
    return pl.pallas_call(
        functools.partial(_attn_kernel, mode=mode, tq=tq, sub=sub, seq=sk, n_sel=n_sel),
        grid=(b, n_pairs, s // ts),
        in_specs=in_specs,
        out_specs=pl.BlockSpec((1, ts, LANES), lambda bi, p, i: (bi, i, p)),
        out_shape=jax.ShapeDtypeStruct((b, s, n_pairs * LANES), BF16),
        scratch_shapes=scratch,
        compiler_params=_params(("parallel", "parallel", "arbitrary")),
        name="attn_" + mode,
    )(q, k, vt, *extra)


def _mla_proj_kernel(x_ref, pos_ref, g_ref, w1_ref, qa_ref, kva_ref, wq_ref, wqr_ref, wk_ref,
                     wvt_ref, invf_ref, qg_ref, kg_ref, mg_ref, q_ref, k_ref, vt_ref, qm_ref, *, tm):
    h = (_rms(x_ref[...]) * g_ref[...]).astype(BF16)
    p = _dot(h, w1_ref[...])
    o = MLA_Q_RANK
    q_lat = p[:, :o]
    kv_lat = p[:, o:o + LANES]
    kr = p[:, o + LANES:o + 2 * LANES]
    kr_rot = p[:, o + 2 * LANES:o + 3 * LANES]
    ym = p[:, o + 3 * LANES:]
    qn = (_rms(q_lat) * qa_ref[...]).astype(BF16)
    kvn = (_rms(kv_lat) * kva_ref[...]).astype(BF16)
    ang = pos_ref[...].astype(F32) * invf_ref[...]
    cos, sin = jnp.cos(ang), jnp.sin(ang)
    k_rope = kr * cos + kr_rot * sin
    width = N_MIX_HEADS * LANES
    for c in range(0, width, 2 * LANES):
        qc = _dot(qn, wq_ref[:, c:c + 2 * LANES])
        qr = _dot(qn, wqr_ref[:, c:c + 2 * LANES])
        kc = _dot(kvn, wk_ref[:, c:c + 2 * LANES])
        for s in range(2):
            sl = slice(s * LANES, (s + 1) * LANES)
            out = slice(c + s * LANES, c + (s + 1) * LANES)
            qq = qc[:, sl] * cos + qr[:, sl] * sin
            q_ref[:, out] = (_slab_norm(qq, qg_ref[...], MLA_QK_DIM)
                             * (MLA_QK_DIM ** -0.5 * LOG2E)).astype(BF16)
            k_ref[:, out] = _slab_norm(kc[:, sl] + k_rope, kg_ref[...], MLA_QK_DIM).astype(BF16)
    _store_vt(vt_ref, _nt_dot(wvt_ref[...], kvn).astype(BF16), tm)
    for s in range(MEM_WIDTH // LANES):
        qm_ref[:, s * LANES:(s + 1) * LANES] = (
            _pair_norm(ym[:, s * LANES:(s + 1) * LANES], mg_ref[...]) * (HEAD_DIM ** -0.5 * LOG2E)
        ).astype(BF16)


def _rot_half(w):
    half = MLA_ROPE_DIM // 2
    return jnp.concatenate([-w[..., half:], w[..., :half]], axis=-1)


def _mla_proj(x2, pos2, seq, g, w_in, qa, kva, w_q_up, w_kv_up, q_gain, k_gain, m_gain):
    t, d = x2.shape
    tm = ROW_TILE
    per_seq = seq // tm
    nh, nope, rope_d = N_MIX_HEADS, MLA_NOPE_DIM, MLA_ROPE_DIM
    pad_r = LANES - nope - rope_d
    o = MLA_Q_RANK + MLA_KV_RANK
    k_r = w_in[:, o:o + rope_d]
    slab = lambda w: jnp.pad(w, ((0, 0), (nope, pad_r)))
    w1 = jnp.concatenate([w_in[:, :o], slab(k_r), slab(_rot_half(k_r)), w_in[:, -MEM_WIDTH:]],
                         axis=1).astype(BF16)
    wq3 = w_q_up.reshape(MLA_Q_RANK, nh, MLA_QK_DIM)
    wq = jnp.pad(wq3, ((0, 0), (0, 0), (0, pad_r))).reshape(MLA_Q_RANK, nh * LANES).astype(BF16)
    wqr = jnp.pad(_rot_half(wq3[..., nope:]), ((0, 0), (0, 0), (nope, pad_r))
                  ).reshape(MLA_Q_RANK, nh * LANES).astype(BF16)
    wkv3 = w_kv_up.reshape(MLA_KV_RANK, nh, nope + MLA_V_DIM)
    wk = jnp.pad(wkv3[..., :nope], ((0, 0), (0, 0), (0, LANES - nope))
                 ).reshape(MLA_KV_RANK, nh * LANES).astype(BF16)
    wvt = wkv3[..., nope:].reshape(MLA_KV_RANK, nh * MLA_V_DIM).T.astype(BF16)
    inv_freq = ROPE_THETA ** (-jnp.arange(0, rope_d, 2, dtype=F32) / rope_d)
    invf = jnp.pad(jnp.tile(inv_freq, 2), (nope, pad_r)).reshape(1, LANES)
    padg = lambda v: jnp.pad(v, (0, pad_r)).reshape(1, LANES)
    const = lambda *shape: pl.BlockSpec(shape, lambda i: (0,) * len(shape))
    rows = lambda w: pl.BlockSpec((tm, w), lambda i: (i, 0))
    qkw = nh * LANES
    return pl.pallas_call(
        functools.partial(_mla_proj_kernel, tm=tm),
        grid=(t // tm,),
        in_specs=[rows(d), rows(1), const(1, d), const(d, w1.shape[1]), const(1, MLA_Q_RANK),
                  const(1, MLA_KV_RANK), const(MLA_Q_RANK, qkw), const(MLA_Q_RANK, qkw),
                  const(MLA_KV_RANK, qkw), const(MIX_WIDTH, MLA_KV_RANK), const(1, LANES),
                  const(1, LANES), const(1, LANES), const(1, LANES)],
        out_specs=[rows(qkw), rows(qkw),
                   pl.BlockSpec((1, tm // KV_BLOCK, MIX_WIDTH, KV_BLOCK),
                                lambda i: (i // per_seq, i % per_seq, 0, 0)),
                   rows(MEM_WIDTH)],
        out_shape=[jax.ShapeDtypeStruct((t, qkw), BF16), jax.ShapeDtypeStruct((t, qkw), BF16),
                   jax.ShapeDtypeStruct((t // seq, seq // KV_BLOCK, MIX_WIDTH, KV_BLOCK), BF16),
                   jax.ShapeDtypeStruct((t, MEM_WIDTH), BF16)],
        compiler_params=_params(("parallel",)),
        name="mla_proj",
    )(x2, pos2, g.reshape(1, d), w1, qa.reshape(1, -1), kva.reshape(1, -1), wq, wqr, wk, wvt, invf,
      padg(q_gain), padg(k_gain), jnp.tile(m_gain, 2).reshape(1, LANES))


def _out_proj_kernel(om_ref, oc_ref, x_ref, w1_ref, w2_ref, o_ref):
    o_ref[...] = x_ref[...] + _dot(om_ref[...], w1_ref[...]) + _dot(oc_ref[...], w2_ref[...])


def _out_proj(o_mix, o_mem, x2, w_out):
    t, d = x2.shape
    tm = ROW_TILE
    rows = lambda w: pl.BlockSpec((tm, w), lambda i: (i, 0))
    const = lambda *shape: pl.BlockSpec(shape, lambda i: (0,) * len(shape))
    return pl.pallas_call(
        _out_proj_kernel, grid=(t // tm,),
        in_specs=[rows(MIX_WIDTH), rows(MEM_WIDTH), rows(d), const(MIX_WIDTH, d), const(MEM_WIDTH, d)],
        out_specs=rows(d), out_shape=jax.ShapeDtypeStruct((t, d), F32),
        compiler_params=_params(("parallel",)),
        name="out_proj",
    )(o_mix, o_mem, x2, w_out[:MIX_WIDTH].astype(BF16), w_out[MIX_WIDTH:].astype(BF16))


def _ffn_kernel(x_ref, halo_ref, g_ref, wup_ref, cw_ref, cb_ref, wdn_ref, o_ref, u_sc, *, tm, per_seq):
    i = pl.program_id(0)
    x = x_ref[...]
    h = (_rms(x) * g_ref[...]).astype(BF16)
    hh = (_rms(halo_ref[...]) * g_ref[...]).astype(BF16)
    keep = jnp.where(i % per_seq == 0, 0.0, 1.0)
    tf = FF_CHUNK
    acc = x
    for c in range(D_FF // tf):
        halves = []
        for part in range(2):
            cols = slice(part * D_FF + c * tf, part * D_FF + (c + 1) * tf)
            w = wup_ref[:, cols]
            u_sc[0:HALO, :] = _dot(hh, w) * keep
            u = _dot(h, w)
            u_sc[HALO:HALO + tm, :] = u
            cw = cw_ref[:, cols]
            halves.append(cw[2:3] * u + cw[1:2] * u_sc[HALO - 1:HALO - 1 + tm, :]
                          + cw[0:1] * u_sc[HALO - 2:HALO - 2 + tm, :] + cb_ref[:, cols])
        gate, val = halves
        act = (gate * jax.nn.sigmoid(gate) * val).astype(BF16)
        acc = acc + _dot(act, wdn_ref[c * tf:(c + 1) * tf, :])
    o_ref[...] = acc


def _ffn(x2, seq, g, w_up, conv_w, conv_b, w_down):
    t, d = x2.shape
    tm = ROW_TILE
    per_seq = seq // tm
    rows = pl.BlockSpec((tm, d), lambda i: (i, 0))
    halo = pl.BlockSpec((HALO, d), lambda i: (jnp.maximum(i * (tm // HALO) - 1, 0), 0))
    resident = lambda *shape: pl.BlockSpec(shape, lambda i: (0,) * len(shape),
                                           pipeline_mode=pl.Buffered(1))
    return pl.pallas_call(
        functools.partial(_ffn_kernel, tm=tm, per_seq=per_seq),
        grid=(t // tm,),
        in_specs=[rows, halo, resident(1, d), resident(d, 2 * D_FF), resident(3, 2 * D_FF),
                  resident(1, 2 * D_FF), resident(D_FF, d)],
        out_specs=rows, out_shape=jax.ShapeDtypeStruct((t, d), F32),
        scratch_shapes=[pltpu.VMEM((HALO + tm, FF_CHUNK), F32)],
        compiler_params=_params(("parallel",)),
        name="conv_ffn",
    )(x2, x2, g.reshape(1, d), w_up.astype(BF16), conv_w, conv_b.reshape(1, -1), w_down.astype(BF16))


def kernel(x, mem, positions, norm_mix, norm_ffn, w_out, mem_norm, w_mem_kv, mem_q_gain,
           mem_k_gain, fox_w_in, fox_b_f, fox_q_gain, fox_k_gain, mla_w_in, mla_qa_norm,
           mla_kva_norm, mla_w_q_up, mla_w_kv_up, mla_q_gain, mla_k_gain, moba_w_in,
           moba_q_gain, moba_k_gain, ffn_w_up, ffn_conv_w, ffn_conv_b, ffn_w_down):
    b, s, d = x.shape
    depth = norm_mix.shape[0]
    assert d == D_MODEL and s % ROW_TILE == 0 and mem.shape[1] == KV_BLOCK
    n_pairs = N_MIX_HEADS // 2
    k_mem, vt_mem = _mem_prep(mem, mem_norm, w_mem_kv, mem_k_gain)
    x2 = x.reshape(b * s, d)
    pos2 = positions.reshape(b * s, 1)
    n_blocks = s // MOBA_BLOCK
    slopes = 2.0 ** (-8.0 * jnp.arange(1, N_MIX_HEADS + 1, dtype=F32) / N_MIX_HEADS)
    slopes = jnp.broadcast_to(slopes.reshape(n_pairs, 2, 1), (n_pairs, 2, LANES))

    for i in range(depth):
        kind, j = i % N_MIXERS, i // N_MIXERS
        if kind == 0:
            q, k, vt, qm, flog = _qkv_proj(x2, s, norm_mix[i], fox_w_in[j], fox_q_gain[j],
                                           fox_k_gain[j], mem_q_gain[i], fox_b_f[j])
            cum = _cumsum(flog.reshape(b, s, LANES))[..., :N_MIX_HEADS]
            cq = cum.transpose(0, 2, 1).reshape(b, n_pairs, 2, s)
            ck = cum.reshape(b, s, n_pairs, 2).transpose(0, 2, 1, 3)
            o_mix = _attention("fox", q.reshape(b, s, -1), k.reshape(b, s, -1), vt, (cq, ck))
        elif kind == 1:
            q, k, vt, qm = _mla_proj(x2, pos2, s, norm_mix[i], mla_w_in[j], mla_qa_norm[j],
                                     mla_kva_norm[j], mla_w_q_up[j], mla_w_kv_up[j],
                                     mla_q_gain[j], mla_k_gain[j], mem_q_gain[i])
            o_mix = _attention("mla", q.reshape(b, s, -1), k.reshape(b, s, -1), vt)
        else:
            q, k, vt, qm = _qkv_proj(x2, s, norm_mix[i], moba_w_in[j], moba_q_gain[j],
                                     moba_k_gain[j], mem_q_gain[i])
            extra = (positions.reshape(b, 1, s), positions.reshape(b, s, 1), slopes)
            o_mix = _attention("moba", q.reshape(b, s, -1), k.reshape(b, s, -1), vt, extra,
                               n_sel=min(MOBA_TOPK, n_blocks - 1))
        o_mem = _attention("mem", qm.reshape(b, s, MEM_WIDTH), k_mem[i], vt_mem)
        x2 = _out_proj(o_mix.reshape(b * s, MIX_WIDTH), o_mem.reshape(b * s, MEM_WIDTH), x2, w_out[i])
        x2 = _ffn(x2, s, norm_ffn[i], ffn_w_up[i], ffn_conv_w[i], ffn_conv_b[i], ffn_w_down[i])
    return x2.reshape(b, s, d)
```

```python
import functools

import jax
import jax.numpy as jnp
from jax import lax
from jax.experimental import pallas as pl
from jax.experimental.pallas import tpu as pltpu

F32 = jnp.float32
BF16 = jnp.bfloat16

D_MODEL = 1024
HEAD_DIM = 64
N_MIX_HEADS = 12
MIX_WIDTH = N_MIX_HEADS * HEAD_DIM
N_MEM_HEADS = 4
MEM_WIDTH = N_MEM_HEADS * HEAD_DIM
N_MIXERS = 3
MLA_Q_RANK = 256
MLA_KV_RANK = 128
MLA_NOPE_DIM = 64
MLA_ROPE_DIM = 32
MLA_QK_DIM = MLA_NOPE_DIM + MLA_ROPE_DIM
MLA_V_DIM = 64
ROPE_THETA = 10000.0
MOBA_BLOCK = 256
MOBA_TOPK = 3
D_FF = 2816
EPS = 1e-6

LANES = 128
KV_BLOCK = 256
Q_TILE = 256
ATTN_SUB = 4
ATTN_LOOKAHEAD = 5
ROW_TILE = 512
FF_CHUNK = 256
HALO = 16
MASKED = -1e30
LOG2E = 1.4426950408889634
DENOM_ROWS = 16
VMEM_LIMIT = 56 * 1024 * 1024


def _params(semantics):
    return pltpu.CompilerParams(dimension_semantics=semantics, vmem_limit_bytes=VMEM_LIMIT)


def _nt_dot(a, b):
    return lax.dot_general(a, b, (((1,), (1,)), ((), ())), preferred_element_type=F32)


def _dot(a, b):
    return jnp.dot(a, b, preferred_element_type=F32)


def _rms(x):
    return x * lax.rsqrt(jnp.mean(x * x, axis=-1, keepdims=True) + EPS)


def _pair_norm(y, gain):
    lane = lax.broadcasted_iota(jnp.int32, y.shape, 1)
    lo = lane < HEAD_DIM
    y2 = y * y
    s_lo = jnp.sum(jnp.where(lo, y2, 0.0), axis=-1, keepdims=True)
    s_hi = jnp.sum(jnp.where(lo, 0.0, y2), axis=-1, keepdims=True)
    ms = jnp.where(lo, s_lo, s_hi) * (1.0 / HEAD_DIM)
    return y * lax.rsqrt(ms + EPS) * gain


def _slab_norm(y, gain, width):
    ms = jnp.sum(y * y, axis=-1, keepdims=True) * (1.0 / width)
    return y * lax.rsqrt(ms + EPS) * gain


def _store_vt(vt_ref, vt, tm):
    for j in range(tm // KV_BLOCK):
        vt_ref[0, j] = vt[:, j * KV_BLOCK:(j + 1) * KV_BLOCK]


def _mem_kernel(mem_ref, g_ref, wk_ref, wvt_ref, kg_ref, kn_ref, vt_ref, *, depth):
    h = (_rms(mem_ref[0]) * g_ref[...]).astype(BF16)
    k = _dot(h, wk_ref[...])
    for s in range(MEM_WIDTH // LANES):
        slab = k[:, s * LANES:(s + 1) * LANES]
        for i in range(depth):
            kn_ref[i, 0, :, s * LANES:(s + 1) * LANES] = _pair_norm(slab, kg_ref[i]).astype(BF16)
    vt_ref[0, 0] = _nt_dot(wvt_ref[...], h).astype(BF16)


def _mem_prep(mem, mem_norm, w_mem_kv, mem_k_gain):
    b, m, d = mem.shape
    depth = mem_k_gain.shape[0]
    wk = w_mem_kv[:, :MEM_WIDTH].astype(BF16)
    wvt = w_mem_kv[:, MEM_WIDTH:].T.astype(BF16)
    kg = jnp.tile(mem_k_gain, (1, 2)).reshape(depth, 1, LANES)
    const = lambda *shape: pl.BlockSpec(shape, lambda i: (0,) * len(shape))
    return pl.pallas_call(
        functools.partial(_mem_kernel, depth=depth),
        grid=(b,),
        in_specs=[pl.BlockSpec((1, m, d), lambda i: (i, 0, 0)), const(1, d), const(d, MEM_WIDTH),
                  const(MEM_WIDTH, d), const(depth, 1, LANES)],
        out_specs=[pl.BlockSpec((depth, 1, m, MEM_WIDTH), lambda i: (0, i, 0, 0)),
                   pl.BlockSpec((1, 1, MEM_WIDTH, m), lambda i: (i, 0, 0, 0))],
        out_shape=[jax.ShapeDtypeStruct((depth, b, m, MEM_WIDTH), BF16),
                   jax.ShapeDtypeStruct((b, 1, MEM_WIDTH, m), BF16)],
        compiler_params=_params(("parallel",)),
        name="mem_prep",
    )(mem, mem_norm.reshape(1, d), wk, wvt, kg)


def _qkv_proj_kernel(x_ref, g_ref, wqk_ref, wvt_ref, wm_ref, qg_ref, kg_ref, mg_ref, *rest,
                     has_gate, tm):
    if has_gate:
        wf_ref, bf_ref, q_ref, k_ref, vt_ref, qm_ref, f_ref = rest
    else:
        q_ref, k_ref, vt_ref, qm_ref = rest
    h = (_rms(x_ref[...]) * g_ref[...]).astype(BF16)
    for c in range(0, 2 * MIX_WIDTH, 2 * LANES):
        y = _dot(h, wqk_ref[:, c:c + 2 * LANES])
        for s in range(2):
            col = c + s * LANES
            slab = y[:, s * LANES:(s + 1) * LANES]
            if col < MIX_WIDTH:
                q_ref[:, col:col + LANES] = (
                    _pair_norm(slab, qg_ref[...]) * (HEAD_DIM ** -0.5 * LOG2E)).astype(BF16)
            else:
                k_ref[:, col - MIX_WIDTH:col - MIX_WIDTH + LANES] = (
                    _pair_norm(slab, kg_ref[...]).astype(BF16))
    _store_vt(vt_ref, _nt_dot(wvt_ref[...], h).astype(BF16), tm)
    ym = _dot(h, wm_ref[...])
    for s in range(MEM_WIDTH // LANES):
        qm_ref[:, s * LANES:(s + 1) * LANES] = (
            _pair_norm(ym[:, s * LANES:(s + 1) * LANES], mg_ref[...]) * (HEAD_DIM ** -0.5 * LOG2E)
        ).astype(BF16)
    if has_gate:
        f = _dot(h, wf_ref[...]) + bf_ref[...]
        f_ref[...] = jnp.minimum(f, 0.0) - jnp.log1p(jnp.exp(-jnp.abs(f)))


def _qkv_proj(x2, seq, g, w_in, q_gain, k_gain, m_gain, b_f=None):
    t, d = x2.shape
    tm = ROW_TILE
    nblk = seq // KV_BLOCK
    per_seq = seq // tm
    has_gate = b_f is not None
    wqk = w_in[:, :2 * MIX_WIDTH].astype(BF16)
    wvt = w_in[:, 2 * MIX_WIDTH:3 * MIX_WIDTH].T.astype(BF16)
    wm = w_in[:, -MEM_WIDTH:].astype(BF16)
    pair = lambda v: jnp.tile(v, 2).reshape(1, LANES)
    const = lambda *shape: pl.BlockSpec(shape, lambda i: (0,) * len(shape))
    rows = lambda w: pl.BlockSpec((tm, w), lambda i: (i, 0))
    args = [x2, g.reshape(1, d), wqk, wvt, wm, pair(q_gain), pair(k_gain), pair(m_gain)]
    in_specs = [rows(d), const(1, d), const(d, 2 * MIX_WIDTH), const(MIX_WIDTH, d),
                const(d, MEM_WIDTH), const(1, LANES), const(1, LANES), const(1, LANES)]
    out_specs = [rows(MIX_WIDTH), rows(MIX_WIDTH),
                 pl.BlockSpec((1, tm // KV_BLOCK, MIX_WIDTH, KV_BLOCK),
                              lambda i: (i // per_seq, i % per_seq, 0, 0)),
                 rows(MEM_WIDTH)]
    out_shape = [jax.ShapeDtypeStruct((t, MIX_WIDTH), BF16), jax.ShapeDtypeStruct((t, MIX_WIDTH), BF16),
                 jax.ShapeDtypeStruct((t // seq, nblk, MIX_WIDTH, KV_BLOCK), BF16),
                 jax.ShapeDtypeStruct((t, MEM_WIDTH), BF16)]
    if has_gate:
        nh = N_MIX_HEADS
        wf = jnp.pad(w_in[:, 3 * MIX_WIDTH:3 * MIX_WIDTH + nh], ((0, 0), (0, LANES - nh))).astype(BF16)
        args += [wf, jnp.pad(b_f, (0, LANES - nh)).reshape(1, LANES)]
        in_specs += [const(d, LANES), const(1, LANES)]
        out_specs.append(rows(LANES))
        out_shape.append(jax.ShapeDtypeStruct((t, LANES), F32))
    return pl.pallas_call(
        functools.partial(_qkv_proj_kernel, has_gate=has_gate, tm=tm),
        grid=(t // tm,), in_specs=in_specs, out_specs=out_specs, out_shape=out_shape,
        compiler_params=_params(("parallel",)),
        name="qkv_proj_gate" if has_gate else "qkv_proj",
    )(*args)


def _cumsum_kernel(f_ref, o_ref, *, seq):
    r = lax.broadcasted_iota(jnp.int32, (KV_BLOCK, KV_BLOCK), 0)
    c = lax.broadcasted_iota(jnp.int32, (KV_BLOCK, KV_BLOCK), 1)
    tri = jnp.where(c <= r, 1.0, 0.0).astype(BF16)
    carry = jnp.zeros((1, LANES), F32)
    for j in range(seq // KV_BLOCK):
        x = f_ref[0, j * KV_BLOCK:(j + 1) * KV_BLOCK, :]
        x1 = x.astype(BF16)
        r1 = x - x1.astype(F32)
        x2 = r1.astype(BF16)
        x3 = (r1 - x2.astype(F32)).astype(BF16)
        cs = _dot(tri, x1) + _dot(tri, x2) + _dot(tri, x3) + carry
        o_ref[0, j * KV_BLOCK:(j + 1) * KV_BLOCK, :] = cs
        carry = cs[KV_BLOCK - 1:KV_BLOCK, :]


def _cumsum(flog):
    b, seq, w = flog.shape
    spec = pl.BlockSpec((1, seq, w), lambda i: (i, 0, 0))
    return pl.pallas_call(
        functools.partial(_cumsum_kernel, seq=seq),
        grid=(b,), in_specs=[spec], out_specs=spec,
        out_shape=jax.ShapeDtypeStruct((b, seq, w), F32),
        compiler_params=_params(("parallel",)),
        name="gate_cumsum",
    )(flog)


def _attn_kernel(*refs, mode, tq, sub, seq, n_sel):
    if mode == "fox":
        q_ref, k_ref, vt_ref, cq_ref, ck_ref, o_ref, m_sc, acc_sc, col_sc = refs
    elif mode == "moba":
        (q_ref, k_ref, vt_ref, pq_ref, pk_ref, sl_ref, o_ref,
         m_sc, acc_sc, col_sc, km_sc, sel_sc) = refs
    else:
        q_ref, k_ref, vt_ref, o_ref, m_sc, acc_sc = refs
    qi = pl.program_id(2)
    tk = KV_BLOCK
    nblk = seq // tk

    if mode == "fox":
        @pl.when(qi == 0)
        def _():
            for h in range(2):
                ck = ck_ref[0, 0, h:h + 1, :] * -LOG2E
                col_sc[h] = jnp.broadcast_to(ck, (LANES, seq)).T
    if mode == "moba":
        pos0 = pk_ref[0, :, 0:1]
        @pl.when(qi == 0)
        def _():
            pk = jnp.broadcast_to((pk_ref[0] - pos0).astype(F32), (LANES, seq)).T
            for h in range(2):
                col_sc[h] = pk * (sl_ref[0, h:h + 1, :] * LOG2E)
            nbp = km_sc.shape[1]
            blk = lax.broadcasted_iota(jnp.int32, (nbp, seq), 0)
            key = lax.broadcasted_iota(jnp.int32, (nbp, seq), 1)
            pool = jnp.where((key >= blk * tk) & (key < (blk + 1) * tk), 1.0, 0.0).astype(BF16)
            km = _dot(pool, k_ref[0]) * (1.0 / tk)
            hi = km.astype(BF16)
            km_sc[0] = hi
            km_sc[1] = (km - hi.astype(F32)).astype(BF16)

    blk0 = qi * sub
    chains = [(i, h) for i in range(sub) for h in range(2)]

    qh, row = {}, {}
    for i in range(sub):
        qb = q_ref[0, i * tq:(i + 1) * tq, :]
        if mode == "mla":
            qh[i, 0], qh[i, 1] = qb[:, :LANES], qb[:, LANES:]
        else:
            qf = qb.astype(F32)
            lane = lax.broadcasted_iota(jnp.int32, qf.shape, 1)
            qh[i, 0] = jnp.where(lane < HEAD_DIM, qf, 0.0).astype(BF16)
            qh[i, 1] = jnp.where(lane >= HEAD_DIM, qf, 0.0).astype(BF16)
        for h in range(2):
            if mode == "fox":
                row[i, h] = cq_ref[0, 0, h:h + 1, i * tq:(i + 1) * tq] * LOG2E
            if mode == "moba":
                pq = (pq_ref[0, :, i * tq:(i + 1) * tq] - pos0).astype(F32)
                row[i, h] = -((sl_ref[0, h:h + 1, 0:1] * LOG2E) * pq)

    if mode == "moba" and n_sel > 0:
        nbp = km_sc.shape[1]
        n_iota = lax.broadcasted_iota(jnp.int32, (nbp, tq), 0)
        for i, h in chains:
            g = _nt_dot(km_sc[0], qh[i, h]) + _nt_dot(km_sc[1], qh[i, h])
            g = jnp.where(n_iota < blk0 + i, g, -jnp.inf)
            bias = jnp.full((nbp, tq), MASKED, F32)
            for _ in range(n_sel):
                mx = jnp.max(g, axis=0, keepdims=True)
                idx = jnp.min(jnp.where(g == mx, n_iota, nbp), axis=0, keepdims=True)
                hit = n_iota == idx
                ok = (mx > -jnp.inf) & (mx < jnp.inf)
                bias = jnp.where(hit & ok, 0.0, bias)
                g = jnp.where(hit, -jnp.inf, g)
            sel_sc[i, h] = bias

    def logits(step):
        i, h, j, own = step
        start = j * tk if isinstance(j, int) else pl.multiple_of(j * tk, tk)
        if mode == "mla":
            kb = k_ref[0, pl.ds(start, tk), h * LANES:(h + 1) * LANES]
        else:
            kb = k_ref[0, pl.ds(start, tk), :]
        z = _nt_dot(kb, qh[i, h])
        if mode in ("fox", "moba"):
            col = col_sc[h, pl.ds(start, tk), :]
            z = (z + row[i, h]) + jnp.concatenate([col] * (tq // LANES), axis=1)
        if own and mode != "mem":
            key = lax.broadcasted_iota(jnp.int32, (tk, tq), 0)
            qry = lax.broadcasted_iota(jnp.int32, (tk, tq), 1)
            z = jnp.where(key <= qry, z, MASKED)
        elif mode == "moba" and n_sel > 0:
            z = z + sel_sc[i, h, pl.ds(j, 1), :]
        return z

    ones_rows = jnp.ones((DENOM_ROWS, tk), BF16)

    def fold(step, z):
        i, h, j, own = step
        vt = jnp.concatenate([vt_ref[0, j, h * HEAD_DIM:(h + 1) * HEAD_DIM, :], ones_rows], axis=0)
        mz = jnp.max(z, axis=0, keepdims=True)
        if own:
            m_sc[i, h] = mz
            acc_sc[i, h] = _dot(vt, jnp.exp2(z - mz).astype(BF16))
        else:
            m_prev = m_sc[i, h]
            m_new = jnp.maximum(m_prev, mz)
            alpha = jnp.exp2(m_prev - m_new)
            acc_sc[i, h] = alpha * acc_sc[i, h] + _dot(vt, jnp.exp2(z - m_new).astype(BF16))
            m_sc[i, h] = m_new

    def run(steps):
        zs = {s: logits(steps[s]) for s in range(min(ATTN_LOOKAHEAD, len(steps)))}
        for s in range(len(steps)):
            if s + ATTN_LOOKAHEAD < len(steps):
                zs[s + ATTN_LOOKAHEAD] = logits(steps[s + ATTN_LOOKAHEAD])
            fold(steps[s], zs.pop(s))

    if mode == "mem":
        run([(i, h, 0, True) for i, h in chains])
    else:
        steps = [(i, h, blk0 + i, True) for i, h in chains]
        steps += [(i, h, blk0 + b, False) for b in range(sub - 1) for i, h in chains if i > b]
        run(steps)

        def body(g, carry):
            run([(i, h, g * sub + b, False) for b in range(sub) for i, h in chains])
            return carry
        lax.fori_loop(0, qi, body, 0)

    for i in range(sub):
        out_t = jnp.concatenate(
            [acc_sc[i, h, :HEAD_DIM, :] / acc_sc[i, h, HEAD_DIM:HEAD_DIM + 1, :] for h in range(2)],
            axis=0)
        o_ref[0, i * tq:(i + 1) * tq, :] = out_t.T.astype(BF16)


def _attention(mode, q, k, vt, extra=(), n_sel=0):
    b, s, _ = q.shape
    sk = k.shape[1]
    n_pairs = vt.shape[2] // LANES
    sub = ATTN_SUB
    tq = Q_TILE
    ts = sub * tq
    assert s % ts == 0
    qw = 2 * LANES if mode == "mla" else LANES
    in_specs = [pl.BlockSpec((1, ts, qw), lambda bi, p, i: (bi, i, p)),
                pl.BlockSpec((1, sk, qw), lambda bi, p, i: (bi, 0, p)),
                pl.BlockSpec((1, sk // KV_BLOCK, LANES, KV_BLOCK), lambda bi, p, i: (bi, 0, p, 0))]
    scratch = [pltpu.VMEM((sub, 2, 1, tq), F32),
               pltpu.VMEM((sub, 2, HEAD_DIM + DENOM_ROWS, tq), F32)]
    if mode == "fox":
        in_specs += [pl.BlockSpec((1, 1, 2, ts), lambda bi, p, i: (bi, p, 0, i)),
                     pl.BlockSpec((1, 1, 2, s), lambda bi, p, i: (bi, p, 0, 0))]
        scratch += [pltpu.VMEM((2, s, LANES), F32)]
    elif mode == "moba":
        nbp = -(-(s // KV_BLOCK) // 16) * 16
        in_specs += [pl.BlockSpec((1, 1, ts), lambda bi, p, i: (bi, 0, i)),
                     pl.BlockSpec((1, 1, s), lambda bi, p, i: (bi, 0, 0)),
                     pl.BlockSpec((1, 2, LANES), lambda bi, p, i: (p, 0, 0))]
        scratch += [pltpu.VMEM((2, s, LANES), F32), pltpu.VMEM((2, nbp, LANES), BF16),
                    pltpu.VMEM((sub, 2, nbp, tq), F32)]
    return pl.pallas_call(
        functools.partial(_attn_kernel, mode=mode, tq=tq, sub=sub, seq=sk, n_sel=n_sel),
        grid=(b, n_pairs, s // ts),
        in_specs=in_specs,
        out_specs=pl.BlockSpec((1, ts, LANES), lambda bi, p, i: (bi, i, p)),
        out_shape=jax.ShapeDtypeStruct((b, s, n_pairs * LANES), BF16),
        scratch_shapes=scratch,
        compiler_params=_params(("parallel", "parallel", "arbitrary")),
        name="attn_" + mode,
    )(q, k, vt, *extra)


def _mla_proj_kernel(x_ref, pos_ref, g_ref, w1_ref, qa_ref, kva_ref, wq_ref, wqr_ref, wk_ref,
                     wvt_ref, invf_ref, qg_ref, kg_ref, mg_ref, q_ref, k_ref, vt_ref, qm_ref, *, tm):
    h = (_rms(x_ref[...]) * g_ref[...]).astype(BF16)
    p = _dot(h, w1_ref[...])
    o = MLA_Q_RANK
    q_lat = p[:, :o]
    kv_lat = p[:, o:o + LANES]
    kr = p[:, o + LANES:o + 2 * LANES]
    kr_rot = p[:, o + 2 * LANES:o + 3 * LANES]
    ym = p[:, o + 3 * LANES:]
    qn = (_rms(q_lat) * qa_ref[...]).astype(BF16)
    kvn = (_rms(kv_lat) * kva_ref[...]).astype(BF16)
    ang = pos_ref[...].astype(F32) * invf_ref[...]
    cos, sin = jnp.cos(ang), jnp.sin(ang)
    k_rope = kr * cos + kr_rot * sin
    width = N_MIX_HEADS * LANES
    for c in range(0, width, 2 * LANES):
        qc = _dot(qn, wq_ref[:, c:c + 2 * LANES])
        qr = _dot(qn, wqr_ref[:, c:c + 2 * LANES])
        kc = _dot(kvn, wk_ref[:, c:c + 2 * LANES])
        for s in range(2):
            sl = slice(s * LANES, (s + 1) * LANES)
            out = slice(c + s * LANES, c + (s + 1) * LANES)
            qq = qc[:, sl] * cos + qr[:, sl] * sin
            q_ref[:, out] = (_slab_norm(qq, qg_ref[...], MLA_QK_DIM)
                             * (MLA_QK_DIM ** -0.5 * LOG2E)).astype(BF16)
            k_ref[:, out] = _slab_norm(kc[:, sl] + k_rope, kg_ref[...], MLA_QK_DIM).astype(BF16)
    _store_vt(vt_ref, _nt_dot(wvt_ref[...], kvn).astype(BF16), tm)
    for s in range(MEM_WIDTH // LANES):
        qm_ref[:, s * LANES:(s + 1) * LANES] = (
            _pair_norm(ym[:, s * LANES:(s + 1) * LANES], mg_ref[...]) * (HEAD_DIM ** -0.5 * LOG2E)
        ).astype(BF16)


def _rot_half(w):
    half = MLA_ROPE_DIM // 2
    return jnp.concatenate([-w[..., half:], w[..., :half]], axis=-1)


def _mla_proj(x2, pos2, seq, g, w_in, qa, kva, w_q_up, w_kv_up, q_gain, k_gain, m_gain):
    t, d = x2.shape
    tm = ROW_TILE
    per_seq = seq // tm
    nh, nope, rope_d = N_MIX_HEADS, MLA_NOPE_DIM, MLA_ROPE_DIM
    pad_r = LANES - nope - rope_d
    o = MLA_Q_RANK + MLA_KV_RANK
    k_r = w_in[:, o:o + rope_d]
    slab = lambda w: jnp.pad(w, ((0, 0), (nope, pad_r)))
    w1 = jnp.concatenate([w_in[:, :o], slab(k_r), slab(_rot_half(k_r)), w_in[:, -MEM_WIDTH:]],
                         axis=1).astype(BF16)
    wq3 = w_q_up.reshape(MLA_Q_RANK, nh, MLA_QK_DIM)
    wq = jnp.pad(wq3, ((0, 0), (0, 0), (0, pad_r))).reshape(MLA_Q_RANK, nh * LANES).astype(BF16)
    wqr = jnp.pad(_rot_half(wq3[..., nope:]), ((0, 0), (0, 0), (nope, pad_r))
                  ).reshape(MLA_Q_RANK, nh * LANES).astype(BF16)
    wkv3 = w_kv_up.reshape(MLA_KV_RANK, nh, nope + MLA_V_DIM)
    wk = jnp.pad(wkv3[..., :nope], ((0, 0), (0, 0), (0, LANES - nope))
                 ).reshape(MLA_KV_RANK, nh * LANES).astype(BF16)
    wvt = wkv3[..., nope:].reshape(MLA_KV_RANK, nh * MLA_V_DIM).T.astype(BF16)
    inv_freq = ROPE_THETA ** (-jnp.arange(0, rope_d, 2, dtype=F32) / rope_d)
    invf = jnp.pad(jnp.tile(inv_freq, 2), (nope, pad_r)).reshape(1, LANES)
    padg = lambda v: jnp.pad(v, (0, pad_r)).reshape(1, LANES)
    const = lambda *shape: pl.BlockSpec(shape, lambda i: (0,) * len(shape))
    rows = lambda w: pl.BlockSpec((tm, w), lambda i: (i, 0))
    qkw = nh * LANES
    return pl.pallas_call(
        functools.partial(_mla_proj_kernel, tm=tm),
        grid=(t // tm,),
        in_specs=[rows(d), rows(1), const(1, d), const(d, w1.shape[1]), const(1, MLA_Q_RANK),
                  const(1, MLA_KV_RANK), const(MLA_Q_RANK, qkw), const(MLA_Q_RANK, qkw),
                  const(MLA_KV_RANK, qkw), const(MIX_WIDTH, MLA_KV_RANK), const(1, LANES),
                  const(1, LANES), const(1, LANES), const(1, LANES)],
        out_specs=[rows(qkw), rows(qkw),
                   pl.BlockSpec((1, tm // KV_BLOCK, MIX_WIDTH, KV_BLOCK),
                                lambda i: (i // per_seq, i % per_seq, 0, 0)),
                   rows(MEM_WIDTH)],
        out_shape=[jax.ShapeDtypeStruct((t, qkw), BF16), jax.ShapeDtypeStruct((t, qkw), BF16),
                   jax.ShapeDtypeStruct((t // seq, seq // KV_BLOCK, MIX_WIDTH, KV_BLOCK), BF16),
                   jax.ShapeDtypeStruct((t, MEM_WIDTH), BF16)],
        compiler_params=_params(("parallel",)),
        name="mla_proj",
    )(x2, pos2, g.reshape(1, d), w1, qa.reshape(1, -1), kva.reshape(1, -1), wq, wqr, wk, wvt, invf,
      padg(q_gain), padg(k_gain), jnp.tile(m_gain, 2).reshape(1, LANES))


def _ffn_kernel(x_ref, xh_ref, om_ref, omh_ref, oc_ref, och_ref, wo1_ref, wo2_ref, g_ref, wup_ref,
                cw_ref, cb_ref, wdn_ref, o_ref, u_sc, act_sc, *, tm, per_seq):
    i = pl.program_id(0)
    x = x_ref[...] + _dot(om_ref[...], wo1_ref[...]) + _dot(oc_ref[...], wo2_ref[...])
    x_halo = xh_ref[...] + _dot(omh_ref[...], wo1_ref[...]) + _dot(och_ref[...], wo2_ref[...])
    h = (_rms(x) * g_ref[...]).astype(BF16)
    hh = (_rms(x_halo) * g_ref[...]).astype(BF16)
    keep = jnp.where(i % per_seq == 0, 0.0, 1.0)
    tf = FF_CHUNK
    n_chunks = D_FF // tf

    def columns(c, part):
        return slice(part * D_FF + c * tf, part * D_FF + (c + 1) * tf)

    def up(c):
        for part in range(2):
            w = wup_ref[:, columns(c, part)]
            u_sc[c % 2, part, 0:HALO, :] = _dot(hh, w) * keep
            u_sc[c % 2, part, HALO:HALO + tm, :] = _dot(h, w)

    def activate(c):
        halves = []
        for part in range(2):
            cw = cw_ref[:, columns(c, part)]
            u = u_sc.at[c % 2, part]
            halves.append(cw[2:3] * u[HALO:HALO + tm, :] + cw[1:2] * u[HALO - 1:HALO - 1 + tm, :]
                          + cw[0:1] * u[HALO - 2:HALO - 2 + tm, :] + cb_ref[:, columns(c, part)])
        gate, val = halves
        act_sc[:, c * tf:(c + 1) * tf] = (gate * jax.nn.sigmoid(gate) * val).astype(BF16)

    up(0)
    for c in range(n_chunks):
        if c + 1 < n_chunks:
            up(c + 1)
        activate(c)
    o_ref[...] = x + _dot(act_sc[...], wdn_ref[...])


def _out_proj_ffn(x2, o_mix, o_mem, seq, w_out, g, w_up, conv_w, conv_b, w_down):
    t, d = x2.shape
    tm = ROW_TILE
    per_seq = seq // tm
    rows = lambda w: pl.BlockSpec((tm, w), lambda i: (i, 0))
    halo = lambda w: pl.BlockSpec((HALO, w), lambda i: (jnp.maximum(i * (tm // HALO) - 1, 0), 0))
    resident = lambda *shape: pl.BlockSpec(shape, lambda i: (0,) * len(shape),
                                           pipeline_mode=pl.Buffered(1))
    return pl.pallas_call(
        functools.partial(_ffn_kernel, tm=tm, per_seq=per_seq),
        grid=(t // tm,),
        in_specs=[rows(d), halo(d), rows(MIX_WIDTH), halo(MIX_WIDTH), rows(MEM_WIDTH), halo(MEM_WIDTH),
                  resident(MIX_WIDTH, d), resident(MEM_WIDTH, d), resident(1, d),
                  resident(d, 2 * D_FF), resident(3, 2 * D_FF), resident(1, 2 * D_FF),
                  resident(D_FF, d)],
        out_specs=rows(d), out_shape=jax.ShapeDtypeStruct((t, d), F32),
        scratch_shapes=[pltpu.VMEM((2, 2, HALO + tm, FF_CHUNK), F32), pltpu.VMEM((tm, D_FF), BF16)],
        compiler_params=_params(("parallel",)),
        name="out_proj_conv_ffn",
    )(x2, x2, o_mix, o_mix, o_mem, o_mem, w_out[:MIX_WIDTH].astype(BF16),
      w_out[MIX_WIDTH:].astype(BF16), g.reshape(1, d), w_up.astype(BF16), conv_w,
      conv_b.reshape(1, -1), w_down.astype(BF16))


def kernel(x, mem, positions, norm_mix, norm_ffn, w_out, mem_norm, w_mem_kv, mem_q_gain,
           mem_k_gain, fox_w_in, fox_b_f, fox_q_gain, fox_k_gain, mla_w_in, mla_qa_norm,
           mla_kva_norm, mla_w_q_up, mla_w_kv_up, mla_q_gain, mla_k_gain, moba_w_in,
           moba_q_gain, moba_k_gain, ffn_w_up, ffn_conv_w, ffn_conv_b, ffn_w_down):
    b, s, d = x.shape
    depth = norm_mix.shape[0]
    assert d == D_MODEL and s % ROW_TILE == 0 and mem.shape[1] == KV_BLOCK
    n_pairs = N_MIX_HEADS // 2
    k_mem, vt_mem = _mem_prep(mem, mem_norm, w_mem_kv, mem_k_gain)
    x2 = x.reshape(b * s, d)
    pos2 = positions.reshape(b * s, 1)
    n_blocks = s // MOBA_BLOCK
    slopes = 2.0 ** (-8.0 * jnp.arange(1, N_MIX_HEADS + 1, dtype=F32) / N_MIX_HEADS)
    slopes = jnp.broadcast_to(slopes.reshape(n_pairs, 2, 1), (n_pairs, 2, LANES))

    for i in range(depth):
        kind, j = i % N_MIXERS, i // N_MIXERS
        if kind == 0:
            q, k, vt, qm, flog = _qkv_proj(x2, s, norm_mix[i], fox_w_in[j], fox_q_gain[j],
                                           fox_k_gain[j], mem_q_gain[i], fox_b_f[j])
            cum = _cumsum(flog.reshape(b, s, LANES))[..., :N_MIX_HEADS]
            cum = cum.transpose(0, 2, 1).reshape(b, n_pairs, 2, s)
            o_mix = _attention("fox", q.reshape(b, s, -1), k.reshape(b, s, -1), vt, (cum, cum))
        elif kind == 1:
            q, k, vt, qm = _mla_proj(x2, pos2, s, norm_mix[i], mla_w_in[j], mla_qa_norm[j],
                                     mla_kva_norm[j], mla_w_q_up[j], mla_w_kv_up[j],
                                     mla_q_gain[j], mla_k_gain[j], mem_q_gain[i])
            o_mix = _attention("mla", q.reshape(b, s, -1), k.reshape(b, s, -1), vt)
        else:
            q, k, vt, qm = _qkv_proj(x2, s, norm_mix[i], moba_w_in[j], moba_q_gain[j],
                                     moba_k_gain[j], mem_q_gain[i])
            extra = (positions.reshape(b, 1, s), positions.reshape(b, 1, s), slopes)
            o_mix = _attention("moba", q.reshape(b, s, -1), k.reshape(b, s, -1), vt, extra,
                               n_sel=min(MOBA_TOPK, n_blocks - 1))
        o_mem = _attention("mem", qm.reshape(b, s, MEM_WIDTH), k_mem[i], vt_mem)
        x2 = _out_proj_ffn(x2, o_mix.reshape(b * s, MIX_WIDTH), o_mem.reshape(b * s, MEM_WIDTH), s,
                           w_out[i], norm_ffn[i], ffn_w_up[i], ffn_conv_w[i], ffn_conv_b[i],
                           ffn_w_down[i])
    return x2.reshape(b, s, d)
```

```python
import functools

import jax
import jax.numpy as jnp
from jax import lax
from jax.experimental import pallas as pl
from jax.experimental.pallas import tpu as pltpu

F32 = jnp.float32
BF16 = jnp.bfloat16

D_MODEL = 1024
HEAD_DIM = 64
N_MIX_HEADS = 12
MIX_WIDTH = N_MIX_HEADS * HEAD_DIM
N_MEM_HEADS = 4
MEM_WIDTH = N_MEM_HEADS * HEAD_DIM
N_MIXERS = 3
MLA_Q_RANK = 256
MLA_KV_RANK = 128
MLA_NOPE_DIM = 64
MLA_ROPE_DIM = 32
MLA_QK_DIM = MLA_NOPE_DIM + MLA_ROPE_DIM
MLA_V_DIM = 64
ROPE_THETA = 10000.0
MOBA_BLOCK = 256
MOBA_TOPK = 3
D_FF = 2816
EPS = 1e-6

LANES = 128
KV_BLOCK = 256
Q_TILE = 256
ATTN_SUB = 4
ATTN_LOOKAHEAD = 5
ROW_TILE = 512
FF_CHUNK = 256
HALO = 16
MASKED = -1e30
LOG2E = 1.4426950408889634
SEL_LANE = 8
DENOM_ROWS = 16
VMEM_LIMIT = 56 * 1024 * 1024


def _params(semantics):
    return pltpu.CompilerParams(dimension_semantics=semantics, vmem_limit_bytes=VMEM_LIMIT)


def _nt_dot(a, b):
    return lax.dot_general(a, b, (((1,), (1,)), ((), ())), preferred_element_type=F32)


def _dot(a, b):
    return jnp.dot(a, b, preferred_element_type=F32)


def _rms(x):
    return x * lax.rsqrt(jnp.mean(x * x, axis=-1, keepdims=True) + EPS)


def _pair_norm(y, gain):
    lane = lax.broadcasted_iota(jnp.int32, y.shape, 1)
    lo = lane < HEAD_DIM
    y2 = y * y
    s_lo = jnp.sum(jnp.where(lo, y2, 0.0), axis=-1, keepdims=True)
    s_hi = jnp.sum(jnp.where(lo, 0.0, y2), axis=-1, keepdims=True)
    ms = jnp.where(lo, s_lo, s_hi) * (1.0 / HEAD_DIM)
    return y * lax.rsqrt(ms + EPS) * gain


def _slab_norm(y, gain, width):
    ms = jnp.sum(y * y, axis=-1, keepdims=True) * (1.0 / width)
    return y * lax.rsqrt(ms + EPS) * gain


def _split3(x):
    x1 = x.astype(BF16).astype(F32)
    r = x - x1
    x2 = r.astype(BF16).astype(F32)
    return x1, x2, (r - x2).astype(BF16).astype(F32)


def _store_vt(vt_ref, vt, tm):
    for j in range(tm // KV_BLOCK):
        vt_ref[0, j] = vt[:, j * KV_BLOCK:(j + 1) * KV_BLOCK]


def _mem_kernel(mem_ref, g_ref, wk_ref, wvt_ref, kg_ref, kn_ref, vt_ref, *, depth):
    h = (_rms(mem_ref[0]) * g_ref[...]).astype(BF16)
    k = _dot(h, wk_ref[...])
    for s in range(MEM_WIDTH // LANES):
        slab = k[:, s * LANES:(s + 1) * LANES]
        for i in range(depth):
            kn_ref[i, 0, :, s * LANES:(s + 1) * LANES] = _pair_norm(slab, kg_ref[i]).astype(BF16)
    vt_ref[0, 0] = _nt_dot(wvt_ref[...], h).astype(BF16)


def _mem_prep(mem, mem_norm, w_mem_kv, mem_k_gain):
    b, m, d = mem.shape
    depth = mem_k_gain.shape[0]
    wk = w_mem_kv[:, :MEM_WIDTH].astype(BF16)
    wvt = w_mem_kv[:, MEM_WIDTH:].T.astype(BF16)
    kg = jnp.tile(mem_k_gain, (1, 2)).reshape(depth, 1, LANES)
    const = lambda *shape: pl.BlockSpec(shape, lambda i: (0,) * len(shape))
    return pl.pallas_call(
        functools.partial(_mem_kernel, depth=depth),
        grid=(b,),
        in_specs=[pl.BlockSpec((1, m, d), lambda i: (i, 0, 0)), const(1, d), const(d, MEM_WIDTH),
                  const(MEM_WIDTH, d), const(depth, 1, LANES)],
        out_specs=[pl.BlockSpec((depth, 1, m, MEM_WIDTH), lambda i: (0, i, 0, 0)),
                   pl.BlockSpec((1, 1, MEM_WIDTH, m), lambda i: (i, 0, 0, 0))],
        out_shape=[jax.ShapeDtypeStruct((depth, b, m, MEM_WIDTH), BF16),
                   jax.ShapeDtypeStruct((b, 1, MEM_WIDTH, m), BF16)],
        compiler_params=_params(("parallel",)),
        name="mem_prep",
    )(mem, mem_norm.reshape(1, d), wk, wvt, kg)


def _qkv_proj_kernel(x_ref, g_ref, wqk_ref, wvt_ref, wm_ref, qg_ref, kg_ref, mg_ref, *rest,
                     has_gate, tm):
    if has_gate:
        wf_ref, bf_ref, q_ref, k_ref, vt_ref, qm_ref, f_ref = rest
    else:
        q_ref, k_ref, vt_ref, qm_ref = rest
    h = (_rms(x_ref[...]) * g_ref[...]).astype(BF16)
    for c in range(0, 2 * MIX_WIDTH, 2 * LANES):
        y = _dot(h, wqk_ref[:, c:c + 2 * LANES])
        for s in range(2):
            col = c + s * LANES
            slab = y[:, s * LANES:(s + 1) * LANES]
            if col < MIX_WIDTH:
                q_ref[:, col:col + LANES] = (
                    _pair_norm(slab, qg_ref[...]) * (HEAD_DIM ** -0.5 * LOG2E)).astype(BF16)
            else:
                k_ref[:, col - MIX_WIDTH:col - MIX_WIDTH + LANES] = (
                    _pair_norm(slab, kg_ref[...]).astype(BF16))
    _store_vt(vt_ref, _nt_dot(wvt_ref[...], h).astype(BF16), tm)
    ym = _dot(h, wm_ref[...])
    for s in range(MEM_WIDTH // LANES):
        qm_ref[:, s * LANES:(s + 1) * LANES] = (
            _pair_norm(ym[:, s * LANES:(s + 1) * LANES], mg_ref[...]) * (HEAD_DIM ** -0.5 * LOG2E)
        ).astype(BF16)
    if has_gate:
        f = _dot(h, wf_ref[...]) + bf_ref[...]
        f_ref[...] = jnp.minimum(f, 0.0) - jnp.log1p(jnp.exp(-jnp.abs(f)))


def _qkv_proj(x2, seq, g, w_in, q_gain, k_gain, m_gain, b_f=None):
    t, d = x2.shape
    tm = ROW_TILE
    nblk = seq // KV_BLOCK
    per_seq = seq // tm
    has_gate = b_f is not None
    wqk = w_in[:, :2 * MIX_WIDTH].astype(BF16)
    wvt = w_in[:, 2 * MIX_WIDTH:3 * MIX_WIDTH].T.astype(BF16)
    wm = w_in[:, -MEM_WIDTH:].astype(BF16)
    pair = lambda v: jnp.tile(v, 2).reshape(1, LANES)
    const = lambda *shape: pl.BlockSpec(shape, lambda i: (0,) * len(shape))
    rows = lambda w: pl.BlockSpec((tm, w), lambda i: (i, 0))
    args = [x2, g.reshape(1, d), wqk, wvt, wm, pair(q_gain), pair(k_gain), pair(m_gain)]
    in_specs = [rows(d), const(1, d), const(d, 2 * MIX_WIDTH), const(MIX_WIDTH, d),
                const(d, MEM_WIDTH), const(1, LANES), const(1, LANES), const(1, LANES)]
    out_specs = [rows(MIX_WIDTH), rows(MIX_WIDTH),
                 pl.BlockSpec((1, tm // KV_BLOCK, MIX_WIDTH, KV_BLOCK),
                              lambda i: (i // per_seq, i % per_seq, 0, 0)),
                 rows(MEM_WIDTH)]
    out_shape = [jax.ShapeDtypeStruct((t, MIX_WIDTH), BF16), jax.ShapeDtypeStruct((t, MIX_WIDTH), BF16),
                 jax.ShapeDtypeStruct((t // seq, nblk, MIX_WIDTH, KV_BLOCK), BF16),
                 jax.ShapeDtypeStruct((t, MEM_WIDTH), BF16)]
    if has_gate:
        nh = N_MIX_HEADS
        wf = jnp.pad(w_in[:, 3 * MIX_WIDTH:3 * MIX_WIDTH + nh], ((0, 0), (0, LANES - nh))).astype(BF16)
        args += [wf, jnp.pad(b_f, (0, LANES - nh)).reshape(1, LANES)]
        in_specs += [const(d, LANES), const(1, LANES)]
        out_specs.append(rows(LANES))
        out_shape.append(jax.ShapeDtypeStruct((t, LANES), F32))
    return pl.pallas_call(
        functools.partial(_qkv_proj_kernel, has_gate=has_gate, tm=tm),
        grid=(t // tm,), in_specs=in_specs, out_specs=out_specs, out_shape=out_shape,
        compiler_params=_params(("parallel",)),
        name="qkv_proj_gate" if has_gate else "qkv_proj",
    )(*args)


def _cumsum_kernel(f_ref, o_ref, *, seq):
    r = lax.broadcasted_iota(jnp.int32, (KV_BLOCK, KV_BLOCK), 0)
    c = lax.broadcasted_iota(jnp.int32, (KV_BLOCK, KV_BLOCK), 1)
    tri = jnp.where(c <= r, 1.0, 0.0).astype(BF16)
    carry = jnp.zeros((1, LANES), F32)
    for j in range(seq // KV_BLOCK):
        x = f_ref[0, j * KV_BLOCK:(j + 1) * KV_BLOCK, :]
        x1 = x.astype(BF16)
        r1 = x - x1.astype(F32)
        x2 = r1.astype(BF16)
        x3 = (r1 - x2.astype(F32)).astype(BF16)
        cs = _dot(tri, x1) + _dot(tri, x2) + _dot(tri, x3) + carry
        o_ref[0, j * KV_BLOCK:(j + 1) * KV_BLOCK, :] = cs
        carry = cs[KV_BLOCK - 1:KV_BLOCK, :]


def _cumsum(flog):
    b, seq, w = flog.shape
    spec = pl.BlockSpec((1, seq, w), lambda i: (i, 0, 0))
    return pl.pallas_call(
        functools.partial(_cumsum_kernel, seq=seq),
        grid=(b,), in_specs=[spec], out_specs=spec,
        out_shape=jax.ShapeDtypeStruct((b, seq, w), F32),
        compiler_params=_params(("parallel",)),
        name="gate_cumsum",
    )(flog)


def _attn_kernel(*refs, mode, tq, sub, seq, n_sel):
    if mode == "fox":
        q_ref, k_ref, vt_ref, cq_ref, ck_ref, o_ref, m_sc, acc_sc, kaug_sc = refs
    elif mode == "moba":
        (q_ref, k_ref, vt_ref, pq_ref, pk_ref, sl_ref, o_ref,
         m_sc, acc_sc, kaug_sc, km_sc) = refs
    else:
        q_ref, k_ref, vt_ref, o_ref, m_sc, acc_sc = refs
    qi = pl.program_id(2)
    tk = KV_BLOCK
    nblk = seq // tk

    def spare(h):
        return HEAD_DIM * (1 - h)

    def own_lanes(h, shape):
        lane = lax.broadcasted_iota(jnp.int32, shape, 1)
        return (lane >= HEAD_DIM * h) & (lane < HEAD_DIM * (h + 1))

    def spare_slab(h, term, term_first, extra):
        n = term.shape[1]
        t1, t2, t3 = _split3(term)
        sl8 = lax.broadcasted_iota(jnp.int32, (8, n), 0)
        t_at, one_at = (0, 3) if term_first else (3, 0)
        head = jnp.where(sl8 == t_at, t1, jnp.where(sl8 == t_at + 1, t2, jnp.where(
            sl8 == t_at + 2, t3, jnp.where((sl8 >= one_at) & (sl8 < one_at + 3), 1.0, 0.0))))
        pieces = [head] if extra is None else [head, extra]
        used = sum(p.shape[0] for p in pieces)
        if spare(h):
            pieces.insert(0, jnp.zeros((spare(h), n), F32))
        pieces.append(jnp.zeros((LANES - spare(h) - used, n), F32))
        return jnp.concatenate(pieces, axis=0).T

    def key_side(h, col, first_key):
        n = col.shape[1]
        extra = None
        if mode == "moba":
            nbp = km_sc.shape[1]
            key = lax.broadcasted_iota(jnp.int32, (nbp, n), 1) + first_key
            blk = lax.shift_right_logical(key, tk.bit_length() - 1)
            extra = jnp.where(blk == lax.broadcasted_iota(jnp.int32, (nbp, n), 0), 1.0, 0.0)
        aug = spare_slab(h, col, True, extra)
        k = k_ref[0, first_key:first_key + n, :].astype(F32)
        return jnp.where(own_lanes(h, k.shape), k, aug).astype(BF16)

    chunk = 4 * tk
    if mode == "fox":
        @pl.when(qi == 0)
        def _():
            for h in range(2):
                for c in range(0, seq, chunk):
                    ck = ck_ref[0, 0, h:h + 1, c:c + chunk] * -LOG2E
                    kaug_sc[h, c:c + chunk, :] = key_side(h, ck, c)
    if mode == "moba":
        pos0 = pk_ref[0, :, 0:1]
        @pl.when(qi == 0)
        def _():
            for c in range(0, seq, chunk):
                pk = (pk_ref[0, :, c:c + chunk] - pos0).astype(F32)
                for h in range(2):
                    col = pk * (sl_ref[0, h:h + 1, 0:1] * LOG2E)
                    kaug_sc[h, c:c + chunk, :] = key_side(h, col, c)
            nbp = km_sc.shape[1]
            blk = lax.broadcasted_iota(jnp.int32, (nbp, seq), 0)
            key = lax.broadcasted_iota(jnp.int32, (nbp, seq), 1)
            pool = jnp.where((key >= blk * tk) & (key < (blk + 1) * tk), 1.0, 0.0).astype(BF16)
            km = _dot(pool, k_ref[0]) * (1.0 / tk)
            hi = km.astype(BF16)
            km_sc[0] = hi
            km_sc[1] = (km - hi.astype(F32)).astype(BF16)

    blk0 = qi * sub
    chains = [(i, h) for i in range(sub) for h in range(2)]

    def selection_bias(i, q_masked):
        nbp = km_sc.shape[1]
        n_iota = lax.broadcasted_iota(jnp.int32, (nbp, tq), 0)
        g = _nt_dot(km_sc[0], q_masked) + _nt_dot(km_sc[1], q_masked)
        g = jnp.where(n_iota < blk0 + i, g, -jnp.inf)
        bias = jnp.where(n_iota == blk0 + i, 0.0, MASKED)
        for _ in range(n_sel):
            mx = jnp.max(g, axis=0, keepdims=True)
            idx = jnp.min(jnp.where(g == mx, n_iota, nbp), axis=0, keepdims=True)
            hit = n_iota == idx
            ok = (mx > -jnp.inf) & (mx < jnp.inf)
            bias = jnp.where(hit & ok, 0.0, bias)
            g = jnp.where(hit, -jnp.inf, g)
        return bias

    def query_side(h, q_masked, row_term, sel):
        aug = spare_slab(h, row_term, False, sel)
        return jnp.where(own_lanes(h, aug.shape), q_masked, aug).astype(BF16)

    qh = {}
    for i in range(sub):
        qb = q_ref[0, i * tq:(i + 1) * tq, :]
        if mode == "mla":
            qh[i, 0], qh[i, 1] = qb[:, :LANES], qb[:, LANES:]
            continue
        qf = qb.astype(F32)
        for h in range(2):
            q_masked = jnp.where(own_lanes(h, qf.shape), qf, 0.0)
            if mode == "mem":
                qh[i, h] = q_masked.astype(BF16)
            elif mode == "fox":
                row_term = cq_ref[0, 0, h:h + 1, i * tq:(i + 1) * tq] * LOG2E
                qh[i, h] = query_side(h, q_masked, row_term, None)
            else:
                pq = (pq_ref[0, :, i * tq:(i + 1) * tq] - pos0).astype(F32)
                row_term = -((sl_ref[0, h:h + 1, 0:1] * LOG2E) * pq)
                sel = selection_bias(i, q_masked.astype(BF16)) if n_sel > 0 else None
                qh[i, h] = query_side(h, q_masked, row_term, sel)

    def logits(step):
        i, h, j, own = step
        start = j * tk if isinstance(j, int) else pl.multiple_of(j * tk, tk)
        if mode == "mla":
            kb = k_ref[0, pl.ds(start, tk), h * LANES:(h + 1) * LANES]
        elif mode == "mem":
            kb = k_ref[0, pl.ds(start, tk), :]
        else:
            kb = kaug_sc[h, pl.ds(start, tk), :]
        z = _nt_dot(kb, qh[i, h])
        if own and mode != "mem":
            key = lax.broadcasted_iota(jnp.int32, (tk, tq), 0)
            qry = lax.broadcasted_iota(jnp.int32, (tk, tq), 1)
            z = jnp.where(key <= qry, z, MASKED)
        return z

    ones_rows = jnp.ones((DENOM_ROWS, tk), BF16)

    def fold(step, z):
        i, h, j, own = step
        vt = jnp.concatenate([vt_ref[0, j, h * HEAD_DIM:(h + 1) * HEAD_DIM, :], ones_rows], axis=0)
        mz = jnp.max(z, axis=0, keepdims=True)
        if own:
            m_sc[i, h] = mz
            acc_sc[i, h] = _dot(vt, jnp.exp2(z - mz).astype(BF16))
        else:
            m_prev = m_sc[i, h]
            m_new = jnp.maximum(m_prev, mz)
            alpha = jnp.exp2(m_prev - m_new)
            acc_sc[i, h] = alpha * acc_sc[i, h] + _dot(vt, jnp.exp2(z - m_new).astype(BF16))
            m_sc[i, h] = m_new

    def run(steps):
        zs = {s: logits(steps[s]) for s in range(min(ATTN_LOOKAHEAD, len(steps)))}
        for s in range(len(steps)):
            if s + ATTN_LOOKAHEAD < len(steps):
                zs[s + ATTN_LOOKAHEAD] = logits(steps[s + ATTN_LOOKAHEAD])
            fold(steps[s], zs.pop(s))

    if mode == "mem":
        run([(i, h, 0, True) for i, h in chains])
    else:
        steps = [(i, h, blk0 + i, True) for i, h in chains]
        steps += [(i, h, blk0 + b, False) for b in range(sub - 1) for i, h in chains if i > b]
        run(steps)

        def body(g, carry):
            run([(i, h, g * sub + b, False) for b in range(sub) for i, h in chains])
            return carry
        lax.fori_loop(0, qi, body, 0)

    for i in range(sub):
        out_t = jnp.concatenate(
            [acc_sc[i, h, :HEAD_DIM, :] / acc_sc[i, h, HEAD_DIM:HEAD_DIM + 1, :] for h in range(2)],
            axis=0)
        o_ref[0, i * tq:(i + 1) * tq, :] = out_t.T.astype(BF16)


def _attention(mode, q, k, vt, extra=(), n_sel=0):
    b, s, _ = q.shape
    sk = k.shape[1]
    n_pairs = vt.shape[2] // LANES
    sub = ATTN_SUB
    tq = Q_TILE
    ts = sub * tq
    assert s % ts == 0
    qw = 2 * LANES if mode == "mla" else LANES
    in_specs = [pl.BlockSpec((1, ts, qw), lambda bi, p, i: (bi, i, p)),
                pl.BlockSpec((1, sk, qw), lambda bi, p, i: (bi, 0, p)),
                pl.BlockSpec((1, sk // KV_BLOCK, LANES, KV_BLOCK), lambda bi, p, i: (bi, 0, p, 0))]
    scratch = [pltpu.VMEM((sub, 2, 1, tq), F32),
               pltpu.VMEM((sub, 2, HEAD_DIM + DENOM_ROWS, tq), F32)]
    if mode == "fox":
        in_specs += [pl.BlockSpec((1, 1, 2, ts), lambda bi, p, i: (bi, p, 0, i)),
                     pl.BlockSpec((1, 1, 2, s), lambda bi, p, i: (bi, p, 0, 0))]
        scratch += [pltpu.VMEM((2, s, LANES), BF16)]
    elif mode == "moba":
        nbp = -(-(s // KV_BLOCK) // 16) * 16
        in_specs += [pl.BlockSpec((1, 1, ts), lambda bi, p, i: (bi, 0, i)),
                     pl.BlockSpec((1, 1, s), lambda bi, p, i: (bi, 0, 0)),
                     pl.BlockSpec((1, 2, LANES), lambda bi, p, i: (p, 0, 0))]
        assert SEL_LANE + nbp <= HEAD_DIM
        scratch += [pltpu.VMEM((2, s, LANES), BF16), pltpu.VMEM((2, nbp, LANES), BF16)]
    return pl.pallas_call(
        functools.partial(_attn_kernel, mode=mode, tq=tq, sub=sub, seq=sk, n_sel=n_sel),
        grid=(b, n_pairs, s // ts),
        in_specs=in_specs,
        out_specs=pl.BlockSpec((1, ts, LANES), lambda bi, p, i: (bi, i, p)),
        out_shape=jax.ShapeDtypeStruct((b, s, n_pairs * LANES), BF16),
        scratch_shapes=scratch,
        compiler_params=_params(("parallel", "parallel", "arbitrary")),
        name="attn_" + mode,
    )(q, k, vt, *extra)


def _rope_table_kernel(pos_ref, invf_ref, cos_ref, sin_ref):
    ang = pos_ref[...].astype(F32) * invf_ref[...]
    cos_ref[...] = jnp.cos(ang)
    sin_ref[...] = jnp.sin(ang)


def _rope_tables(positions):
    t = positions.size
    half = MLA_ROPE_DIM // 2
    per_row = LANES // half
    inv_freq = ROPE_THETA ** (-jnp.arange(0, MLA_ROPE_DIM, 2, dtype=F32) / MLA_ROPE_DIM)
    pos = jnp.repeat(positions.reshape(t // per_row, per_row), half, axis=1)
    rows = min(t // per_row, 1024)
    spec = pl.BlockSpec((rows, LANES), lambda i: (i, 0))
    cos, sin = pl.pallas_call(
        _rope_table_kernel, grid=(t // per_row // rows,),
        in_specs=[spec, pl.BlockSpec((1, LANES), lambda i: (0, 0))],
        out_specs=[spec, spec],
        out_shape=[jax.ShapeDtypeStruct((t // per_row, LANES), F32)] * 2,
        compiler_params=_params(("parallel",)),
        name="rope_tables",
    )(pos, jnp.tile(inv_freq, per_row).reshape(1, LANES))
    pad_r = LANES - MLA_NOPE_DIM - MLA_ROPE_DIM
    slab = lambda v, fill: jnp.concatenate(
        [jnp.full((t, MLA_NOPE_DIM), fill, F32), v.reshape(t, half), v.reshape(t, half),
         jnp.full((t, pad_r), fill, F32)], axis=1)
    return slab(cos, 1.0), slab(sin, 0.0)


def _mla_proj_kernel(x_ref, cos_ref, sin_ref, g_ref, w1_ref, qa_ref, kva_ref, wq_ref, wqr_ref, wk_ref,
                     wvt_ref, qg_ref, kg_ref, mg_ref, q_ref, k_ref, vt_ref, qm_ref, *, tm):
    h = (_rms(x_ref[...]) * g_ref[...]).astype(BF16)
    p = _dot(h, w1_ref[...])
    o = MLA_Q_RANK
    q_lat = p[:, :o]
    kv_lat = p[:, o:o + LANES]
    kr = p[:, o + LANES:o + 2 * LANES]
    kr_rot = p[:, o + 2 * LANES:o + 3 * LANES]
    ym = p[:, o + 3 * LANES:]
    qn = (_rms(q_lat) * qa_ref[...]).astype(BF16)
    kvn = (_rms(kv_lat) * kva_ref[...]).astype(BF16)
    cos, sin = cos_ref[...], sin_ref[...]
    k_rope = kr * cos + kr_rot * sin
    width = N_MIX_HEADS * LANES
    for c in range(0, width, 2 * LANES):
        qc = _dot(qn, wq_ref[:, c:c + 2 * LANES])
        qr = _dot(qn, wqr_ref[:, c:c + 2 * LANES])
        kc = _dot(kvn, wk_ref[:, c:c + 2 * LANES])
        for s in range(2):
            sl = slice(s * LANES, (s + 1) * LANES)
            out = slice(c + s * LANES, c + (s + 1) * LANES)
            qq = qc[:, sl] * cos + qr[:, sl] * sin
            q_ref[:, out] = (_slab_norm(qq, qg_ref[...], MLA_QK_DIM)
                             * (MLA_QK_DIM ** -0.5 * LOG2E)).astype(BF16)
            k_ref[:, out] = _slab_norm(kc[:, sl] + k_rope, kg_ref[...], MLA_QK_DIM).astype(BF16)
    _store_vt(vt_ref, _nt_dot(wvt_ref[...], kvn).astype(BF16), tm)
    for s in range(MEM_WIDTH // LANES):
        qm_ref[:, s * LANES:(s + 1) * LANES] = (
            _pair_norm(ym[:, s * LANES:(s + 1) * LANES], mg_ref[...]) * (HEAD_DIM ** -0.5 * LOG2E)
        ).astype(BF16)


def _rot_half(w):
    half = MLA_ROPE_DIM // 2
    return jnp.concatenate([-w[..., half:], w[..., :half]], axis=-1)


def _mla_proj(x2, cos, sin, seq, g, w_in, qa, kva, w_q_up, w_kv_up, q_gain, k_gain, m_gain):
    t, d = x2.shape
    tm = ROW_TILE
    per_seq = seq // tm
    nh, nope, rope_d = N_MIX_HEADS, MLA_NOPE_DIM, MLA_ROPE_DIM
    pad_r = LANES - nope - rope_d
    o = MLA_Q_RANK + MLA_KV_RANK
    k_r = w_in[:, o:o + rope_d]
    slab = lambda w: jnp.pad(w, ((0, 0), (nope, pad_r)))
    w1 = jnp.concatenate([w_in[:, :o], slab(k_r), slab(_rot_half(k_r)), w_in[:, -MEM_WIDTH:]],
                         axis=1).astype(BF16)
    wq3 = w_q_up.reshape(MLA_Q_RANK, nh, MLA_QK_DIM)
    wq = jnp.pad(wq3, ((0, 0), (0, 0), (0, pad_r))).reshape(MLA_Q_RANK, nh * LANES).astype(BF16)
    wqr = jnp.pad(_rot_half(wq3[..., nope:]), ((0, 0), (0, 0), (nope, pad_r))
                  ).reshape(MLA_Q_RANK, nh * LANES).astype(BF16)
    wkv3 = w_kv_up.reshape(MLA_KV_RANK, nh, nope + MLA_V_DIM)
    wk = jnp.pad(wkv3[..., :nope], ((0, 0), (0, 0), (0, LANES - nope))
                 ).reshape(MLA_KV_RANK, nh * LANES).astype(BF16)
    wvt = wkv3[..., nope:].reshape(MLA_KV_RANK, nh * MLA_V_DIM).T.astype(BF16)
    padg = lambda v: jnp.pad(v, (0, pad_r)).reshape(1, LANES)
    const = lambda *shape: pl.BlockSpec(shape, lambda i: (0,) * len(shape))
    rows = lambda w: pl.BlockSpec((tm, w), lambda i: (i, 0))
    qkw = nh * LANES
    return pl.pallas_call(
        functools.partial(_mla_proj_kernel, tm=tm),
        grid=(t // tm,),
        in_specs=[rows(d), rows(LANES), rows(LANES), const(1, d), const(d, w1.shape[1]),
                  const(1, MLA_Q_RANK), const(1, MLA_KV_RANK), const(MLA_Q_RANK, qkw),
                  const(MLA_Q_RANK, qkw), const(MLA_KV_RANK, qkw), const(MIX_WIDTH, MLA_KV_RANK),
                  const(1, LANES), const(1, LANES), const(1, LANES)],
        out_specs=[rows(qkw), rows(qkw),
                   pl.BlockSpec((1, tm // KV_BLOCK, MIX_WIDTH, KV_BLOCK),
                                lambda i: (i // per_seq, i % per_seq, 0, 0)),
                   rows(MEM_WIDTH)],
        out_shape=[jax.ShapeDtypeStruct((t, qkw), BF16), jax.ShapeDtypeStruct((t, qkw), BF16),
                   jax.ShapeDtypeStruct((t // seq, seq // KV_BLOCK, MIX_WIDTH, KV_BLOCK), BF16),
                   jax.ShapeDtypeStruct((t, MEM_WIDTH), BF16)],
        compiler_params=_params(("parallel",)),
        name="mla_proj",
    )(x2, cos, sin, g.reshape(1, d), w1, qa.reshape(1, -1), kva.reshape(1, -1), wq, wqr, wk, wvt,
      padg(q_gain), padg(k_gain), jnp.tile(m_gain, 2).reshape(1, LANES))


def _ffn_kernel(x_ref, xh_ref, om_ref, omh_ref, oc_ref, och_ref, wo1_ref, wo2_ref, g_ref, wup_ref,
                cw_ref, cb_ref, wdn_ref, o_ref, u_sc, act_sc, *, tm, per_seq):
    i = pl.program_id(0)
    x = x_ref[...] + _dot(om_ref[...], wo1_ref[...]) + _dot(oc_ref[...], wo2_ref[...])
    x_halo = xh_ref[...] + _dot(omh_ref[...], wo1_ref[...]) + _dot(och_ref[...], wo2_ref[...])
    h = (_rms(x) * g_ref[...]).astype(BF16)
    hh = (_rms(x_halo) * g_ref[...]).astype(BF16)
    keep = jnp.where(i % per_seq == 0, 0.0, 1.0)
    tf = FF_CHUNK
    n_chunks = D_FF // tf

    def columns(c, part):
        return slice(part * D_FF + c * tf, part * D_FF + (c + 1) * tf)

    def up(c):
        for part in range(2):
            w = wup_ref[:, columns(c, part)]
            u_sc[c % 2, part, 0:HALO, :] = _dot(hh, w) * keep
            u_sc[c % 2, part, HALO:HALO + tm, :] = _dot(h, w)

    def activate(c):
        halves = []
        for part in range(2):
            cw = cw_ref[:, columns(c, part)]
            u = u_sc.at[c % 2, part]
            halves.append(cw[2:3] * u[HALO:HALO + tm, :] + cw[1:2] * u[HALO - 1:HALO - 1 + tm, :]
                          + cw[0:1] * u[HALO - 2:HALO - 2 + tm, :] + cb_ref[:, columns(c, part)])
        gate, val = halves
        act_sc[:, c * tf:(c + 1) * tf] = (gate * jax.nn.sigmoid(gate) * val).astype(BF16)

    up(0)
    for c in range(n_chunks):
        if c + 1 < n_chunks:
            up(c + 1)
        activate(c)
    o_ref[...] = x + _dot(act_sc[...], wdn_ref[...])


def _out_proj_ffn(x2, o_mix, o_mem, seq, w_out, g, w_up, conv_w, conv_b, w_down):
    t, d = x2.shape
    tm = ROW_TILE
    per_seq = seq // tm
    rows = lambda w: pl.BlockSpec((tm, w), lambda i: (i, 0))
    halo = lambda w: pl.BlockSpec((HALO, w), lambda i: (jnp.maximum(i * (tm // HALO) - 1, 0), 0))
    resident = lambda *shape: pl.BlockSpec(shape, lambda i: (0,) * len(shape),
                                           pipeline_mode=pl.Buffered(1))
    return pl.pallas_call(
        functools.partial(_ffn_kernel, tm=tm, per_seq=per_seq),
        grid=(t // tm,),
        in_specs=[rows(d), halo(d), rows(MIX_WIDTH), halo(MIX_WIDTH), rows(MEM_WIDTH), halo(MEM_WIDTH),
                  resident(MIX_WIDTH, d), resident(MEM_WIDTH, d), resident(1, d),
                  resident(d, 2 * D_FF), resident(3, 2 * D_FF), resident(1, 2 * D_FF),
                  resident(D_FF, d)],
        out_specs=rows(d), out_shape=jax.ShapeDtypeStruct((t, d), F32),
        scratch_shapes=[pltpu.VMEM((2, 2, HALO + tm, FF_CHUNK), F32), pltpu.VMEM((tm, D_FF), BF16)],
        compiler_params=_params(("parallel",)),
        name="out_proj_conv_ffn",
    )(x2, x2, o_mix, o_mix, o_mem, o_mem, w_out[:MIX_WIDTH].astype(BF16),
      w_out[MIX_WIDTH:].astype(BF16), g.reshape(1, d), w_up.astype(BF16), conv_w,
      conv_b.reshape(1, -1), w_down.astype(BF16))


def kernel(x, mem, positions, norm_mix, norm_ffn, w_out, mem_norm, w_mem_kv, mem_q_gain,
           mem_k_gain, fox_w_in, fox_b_f, fox_q_gain, fox_k_gain, mla_w_in, mla_qa_norm,
           mla_kva_norm, mla_w_q_up, mla_w_kv_up, mla_q_gain, mla_k_gain, moba_w_in,
           moba_q_gain, moba_k_gain, ffn_w_up, ffn_conv_w, ffn_conv_b, ffn_w_down):
    b, s, d = x.shape
    depth = norm_mix.shape[0]
    assert d == D_MODEL and s % ROW_TILE == 0 and mem.shape[1] == KV_BLOCK
    n_pairs = N_MIX_HEADS // 2
    k_mem, vt_mem = _mem_prep(mem, mem_norm, w_mem_kv, mem_k_gain)
    x2 = x.reshape(b * s, d)
    rope_cos, rope_sin = _rope_tables(positions) if depth > 1 else (None, None)
    n_blocks = s // MOBA_BLOCK
    slopes = 2.0 ** (-8.0 * jnp.arange(1, N_MIX_HEADS + 1, dtype=F32) / N_MIX_HEADS)
    slopes = jnp.broadcast_to(slopes.reshape(n_pairs, 2, 1), (n_pairs, 2, LANES))

    for i in range(depth):
        kind, j = i % N_MIXERS, i // N_MIXERS
        if kind == 0:
            q, k, vt, qm, flog = _qkv_proj(x2, s, norm_mix[i], fox_w_in[j], fox_q_gain[j],
                                           fox_k_gain[j], mem_q_gain[i], fox_b_f[j])
            cum = _cumsum(flog.reshape(b, s, LANES))[..., :N_MIX_HEADS]
            cum = cum.transpose(0, 2, 1).reshape(b, n_pairs, 2, s)
            o_mix = _attention("fox", q.reshape(b, s, -1), k.reshape(b, s, -1), vt, (cum, cum))
        elif kind == 1:
            q, k, vt, qm = _mla_proj(x2, rope_cos, rope_sin, s, norm_mix[i], mla_w_in[j], mla_qa_norm[j],
                                     mla_kva_norm[j], mla_w_q_up[j], mla_w_kv_up[j],
                                     mla_q_gain[j], mla_k_gain[j], mem_q_gain[i])
            o_mix = _attention("mla", q.reshape(b, s, -1), k.reshape(b, s, -1), vt)
        else:
            q, k, vt, qm = _qkv_proj(x2, s, norm_mix[i], moba_w_in[j], moba_q_gain[j],
                                     moba_k_gain[j], mem_q_gain[i])
            extra = (positions.reshape(b, 1, s), positions.reshape(b, 1, s), slopes)
            o_mix = _attention("moba", q.reshape(b, s, -1), k.reshape(b, s, -1), vt, extra,
                               n_sel=min(MOBA_TOPK, n_blocks - 1))
        o_mem = _attention("mem", qm.reshape(b, s, MEM_WIDTH), k_mem[i], vt_mem)
        x2 = _out_proj_ffn(x2, o_mix.reshape(b * s, MIX_WIDTH), o_mem.reshape(b * s, MEM_WIDTH), s,
                           w_out[i], norm_ffn[i], ffn_w_up[i], ffn_conv_w[i], ffn_conv_b[i],
                           ffn_w_down[i])
    return x2.reshape(b, s, d)
```

```python
import functools

import jax
import jax.numpy as jnp
from jax import lax
from jax.experimental import pallas as pl
from jax.experimental.pallas import tpu as pltpu

F32 = jnp.float32
BF16 = jnp.bfloat16

D_MODEL = 1024
HEAD_DIM = 64
N_MIX_HEADS = 12
MIX_WIDTH = N_MIX_HEADS * HEAD_DIM
N_MEM_HEADS = 4
MEM_WIDTH = N_MEM_HEADS * HEAD_DIM
N_MIXERS = 3
MLA_Q_RANK = 256
MLA_KV_RANK = 128
MLA_NOPE_DIM = 64
MLA_ROPE_DIM = 32
MLA_QK_DIM = MLA_NOPE_DIM + MLA_ROPE_DIM
MLA_V_DIM = 64
ROPE_THETA = 10000.0
MOBA_BLOCK = 256
MOBA_TOPK = 3
D_FF = 2816
EPS = 1e-6

LANES = 128
KV_BLOCK = 256
Q_TILE = 256
ATTN_SUB = 8
ATTN_LOOKAHEAD = 5
ROW_TILE = 512
FF_CHUNK = 256
HALO = 16
MASKED = -1e30
LOG2E = 1.4426950408889634
SEL_LANE = 8
DENOM_ROWS = 16
VMEM_LIMIT = 56 * 1024 * 1024


def _params(semantics):
    return pltpu.CompilerParams(dimension_semantics=semantics, vmem_limit_bytes=VMEM_LIMIT)


def _nt_dot(a, b):
    return lax.dot_general(a, b, (((1,), (1,)), ((), ())), preferred_element_type=F32)


def _dot(a, b):
    return jnp.dot(a, b, preferred_element_type=F32)


def _rms(x):
    return x * lax.rsqrt(jnp.mean(x * x, axis=-1, keepdims=True) + EPS)


def _pair_norm(y, gain):
    lane = lax.broadcasted_iota(jnp.int32, y.shape, 1)
    lo = lane < HEAD_DIM
    y2 = y * y
    s_lo = jnp.sum(jnp.where(lo, y2, 0.0), axis=-1, keepdims=True)
    s_hi = jnp.sum(jnp.where(lo, 0.0, y2), axis=-1, keepdims=True)
    ms = jnp.where(lo, s_lo, s_hi) * (1.0 / HEAD_DIM)
    return y * lax.rsqrt(ms + EPS) * gain


def _slab_norm(y, gain, width):
    ms = jnp.sum(y * y, axis=-1, keepdims=True) * (1.0 / width)
    return y * lax.rsqrt(ms + EPS) * gain


def _split3(x):
    x1 = x.astype(BF16).astype(F32)
    r = x - x1
    x2 = r.astype(BF16).astype(F32)
    return x1, x2, (r - x2).astype(BF16).astype(F32)


def _store_vt(vt_ref, vt, tm):
    for j in range(tm // KV_BLOCK):
        vt_ref[0, j] = vt[:, j * KV_BLOCK:(j + 1) * KV_BLOCK]


def _mem_kernel(mem_ref, g_ref, wk_ref, wvt_ref, kg_ref, kn_ref, vt_ref, *, depth):
    h = (_rms(mem_ref[0]) * g_ref[...]).astype(BF16)
    k = _dot(h, wk_ref[...])
    for s in range(MEM_WIDTH // LANES):
        slab = k[:, s * LANES:(s + 1) * LANES]
        for i in range(depth):
            kn_ref[i, 0, :, s * LANES:(s + 1) * LANES] = _pair_norm(slab, kg_ref[i]).astype(BF16)
    vt_ref[0, 0] = _nt_dot(wvt_ref[...], h).astype(BF16)


def _mem_prep(mem, mem_norm, w_mem_kv, mem_k_gain):
    b, m, d = mem.shape
    depth = mem_k_gain.shape[0]
    wk = w_mem_kv[:, :MEM_WIDTH].astype(BF16)
    wvt = w_mem_kv[:, MEM_WIDTH:].T.astype(BF16)
    kg = jnp.tile(mem_k_gain, (1, 2)).reshape(depth, 1, LANES)
    const = lambda *shape: pl.BlockSpec(shape, lambda i: (0,) * len(shape))
    return pl.pallas_call(
        functools.partial(_mem_kernel, depth=depth),
        grid=(b,),
        in_specs=[pl.BlockSpec((1, m, d), lambda i: (i, 0, 0)), const(1, d), const(d, MEM_WIDTH),
                  const(MEM_WIDTH, d), const(depth, 1, LANES)],
        out_specs=[pl.BlockSpec((depth, 1, m, MEM_WIDTH), lambda i: (0, i, 0, 0)),
                   pl.BlockSpec((1, 1, MEM_WIDTH, m), lambda i: (i, 0, 0, 0))],
        out_shape=[jax.ShapeDtypeStruct((depth, b, m, MEM_WIDTH), BF16),
                   jax.ShapeDtypeStruct((b, 1, MEM_WIDTH, m), BF16)],
        compiler_params=_params(("parallel",)),
        name="mem_prep",
    )(mem, mem_norm.reshape(1, d), wk, wvt, kg)


def _qkv_proj_kernel(x_ref, g_ref, wqk_ref, wvt_ref, wm_ref, qg_ref, kg_ref, mg_ref, *rest,
                     has_gate, tm):
    if has_gate:
        wf_ref, bf_ref, q_ref, k_ref, vt_ref, qm_ref, f_ref = rest
    else:
        q_ref, k_ref, vt_ref, qm_ref = rest
    h = (_rms(x_ref[...]) * g_ref[...]).astype(BF16)
    for c in range(0, 2 * MIX_WIDTH, 2 * LANES):
        y = _dot(h, wqk_ref[:, c:c + 2 * LANES])
        for s in range(2):
            col = c + s * LANES
            slab = y[:, s * LANES:(s + 1) * LANES]
            if col < MIX_WIDTH:
                q_ref[:, col:col + LANES] = (
                    _pair_norm(slab, qg_ref[...]) * (HEAD_DIM ** -0.5 * LOG2E)).astype(BF16)
            else:
                k_ref[:, col - MIX_WIDTH:col - MIX_WIDTH + LANES] = (
                    _pair_norm(slab, kg_ref[...]).astype(BF16))
    _store_vt(vt_ref, _nt_dot(wvt_ref[...], h).astype(BF16), tm)
    ym = _dot(h, wm_ref[...])
    for s in range(MEM_WIDTH // LANES):
        qm_ref[:, s * LANES:(s + 1) * LANES] = (
            _pair_norm(ym[:, s * LANES:(s + 1) * LANES], mg_ref[...]) * (HEAD_DIM ** -0.5 * LOG2E)
        ).astype(BF16)
    if has_gate:
        f = _dot(h, wf_ref[...]) + bf_ref[...]
        f_ref[...] = jnp.minimum(f, 0.0) - jnp.log1p(jnp.exp(-jnp.abs(f)))


def _qkv_proj(x2, seq, g, w_in, q_gain, k_gain, m_gain, b_f=None):
    t, d = x2.shape
    tm = ROW_TILE
    nblk = seq // KV_BLOCK
    per_seq = seq // tm
    has_gate = b_f is not None
    wqk = w_in[:, :2 * MIX_WIDTH].astype(BF16)
    wvt = w_in[:, 2 * MIX_WIDTH:3 * MIX_WIDTH].T.astype(BF16)
    wm = w_in[:, -MEM_WIDTH:].astype(BF16)
    pair = lambda v: jnp.tile(v, 2).reshape(1, LANES)
    const = lambda *shape: pl.BlockSpec(shape, lambda i: (0,) * len(shape))
    rows = lambda w: pl.BlockSpec((tm, w), lambda i: (i, 0))
    args = [x2, g.reshape(1, d), wqk, wvt, wm, pair(q_gain), pair(k_gain), pair(m_gain)]
    in_specs = [rows(d), const(1, d), const(d, 2 * MIX_WIDTH), const(MIX_WIDTH, d),
                const(d, MEM_WIDTH), const(1, LANES), const(1, LANES), const(1, LANES)]
    out_specs = [rows(MIX_WIDTH), rows(MIX_WIDTH),
                 pl.BlockSpec((1, tm // KV_BLOCK, MIX_WIDTH, KV_BLOCK),
                              lambda i: (i // per_seq, i % per_seq, 0, 0)),
                 rows(MEM_WIDTH)]
    out_shape = [jax.ShapeDtypeStruct((t, MIX_WIDTH), BF16), jax.ShapeDtypeStruct((t, MIX_WIDTH), BF16),
                 jax.ShapeDtypeStruct((t // seq, nblk, MIX_WIDTH, KV_BLOCK), BF16),
                 jax.ShapeDtypeStruct((t, MEM_WIDTH), BF16)]
    if has_gate:
        nh = N_MIX_HEADS
        wf = jnp.pad(w_in[:, 3 * MIX_WIDTH:3 * MIX_WIDTH + nh], ((0, 0), (0, LANES - nh))).astype(BF16)
        args += [wf, jnp.pad(b_f, (0, LANES - nh)).reshape(1, LANES)]
        in_specs += [const(d, LANES), const(1, LANES)]
        out_specs.append(rows(LANES))
        out_shape.append(jax.ShapeDtypeStruct((t, LANES), F32))
    return pl.pallas_call(
        functools.partial(_qkv_proj_kernel, has_gate=has_gate, tm=tm),
        grid=(t // tm,), in_specs=in_specs, out_specs=out_specs, out_shape=out_shape,
        compiler_params=_params(("parallel",)),
        name="qkv_proj_gate" if has_gate else "qkv_proj",
    )(*args)


def _cumsum_kernel(f_ref, o_ref, *, seq):
    r = lax.broadcasted_iota(jnp.int32, (KV_BLOCK, KV_BLOCK), 0)
    c = lax.broadcasted_iota(jnp.int32, (KV_BLOCK, KV_BLOCK), 1)
    tri = jnp.where(c <= r, 1.0, 0.0).astype(BF16)
    carry = jnp.zeros((1, LANES), F32)
    for j in range(seq // KV_BLOCK):
        x = f_ref[0, j * KV_BLOCK:(j + 1) * KV_BLOCK, :]
        x1 = x.astype(BF16)
        r1 = x - x1.astype(F32)
        x2 = r1.astype(BF16)
        x3 = (r1 - x2.astype(F32)).astype(BF16)
        cs = _dot(tri, x1) + _dot(tri, x2) + _dot(tri, x3) + carry
        o_ref[0, j * KV_BLOCK:(j + 1) * KV_BLOCK, :] = cs
        carry = cs[KV_BLOCK - 1:KV_BLOCK, :]


def _cumsum(flog):
    b, seq, w = flog.shape
    spec = pl.BlockSpec((1, seq, w), lambda i: (i, 0, 0))
    return pl.pallas_call(
        functools.partial(_cumsum_kernel, seq=seq),
        grid=(b,), in_specs=[spec], out_specs=spec,
        out_shape=jax.ShapeDtypeStruct((b, seq, w), F32),
        compiler_params=_params(("parallel",)),
        name="gate_cumsum",
    )(flog)


def _attn_kernel(*refs, mode, tq, sub, seq, n_sel):
    if mode == "fox":
        q_ref, k_ref, vt_ref, cq_ref, ck_ref, o_ref, m_sc, acc_sc, kaug_sc = refs
    elif mode == "moba":
        (q_ref, k_ref, vt_ref, pq_ref, pk_ref, sl_ref, o_ref,
         m_sc, acc_sc, kaug_sc, km_sc) = refs
    else:
        q_ref, k_ref, vt_ref, o_ref, m_sc, acc_sc = refs
    qi = pl.program_id(2)
    tk = KV_BLOCK
    nblk = seq // tk

    def spare(h):
        return HEAD_DIM * (1 - h)

    def own_lanes(h, shape):
        lane = lax.broadcasted_iota(jnp.int32, shape, 1)
        return (lane >= HEAD_DIM * h) & (lane < HEAD_DIM * (h + 1))

    def spare_slab(h, term, term_first, extra):
        n = term.shape[1]
        t1, t2, t3 = _split3(term)
        sl8 = lax.broadcasted_iota(jnp.int32, (8, n), 0)
        t_at, one_at = (0, 3) if term_first else (3, 0)
        head = jnp.where(sl8 == t_at, t1, jnp.where(sl8 == t_at + 1, t2, jnp.where(
            sl8 == t_at + 2, t3, jnp.where((sl8 >= one_at) & (sl8 < one_at + 3), 1.0, 0.0))))
        pieces = [head] if extra is None else [head, extra]
        used = sum(p.shape[0] for p in pieces)
        if spare(h):
            pieces.insert(0, jnp.zeros((spare(h), n), F32))
        pieces.append(jnp.zeros((LANES - spare(h) - used, n), F32))
        return jnp.concatenate(pieces, axis=0).T

    def key_side(h, col, first_key):
        n = col.shape[1]
        extra = None
        if mode == "moba":
            nbp = km_sc.shape[1]
            key = lax.broadcasted_iota(jnp.int32, (nbp, n), 1) + first_key
            blk = lax.shift_right_logical(key, tk.bit_length() - 1)
            extra = jnp.where(blk == lax.broadcasted_iota(jnp.int32, (nbp, n), 0), 1.0, 0.0)
        aug = spare_slab(h, col, True, extra)
        k = k_ref[0, first_key:first_key + n, :].astype(F32)
        return jnp.where(own_lanes(h, k.shape), k, aug).astype(BF16)

    chunk = 4 * tk
    if mode == "fox":
        @pl.when(qi == 0)
        def _():
            for h in range(2):
                for c in range(0, seq, chunk):
                    ck = ck_ref[0, 0, h:h + 1, c:c + chunk] * -LOG2E
                    kaug_sc[h, c:c + chunk, :] = key_side(h, ck, c)
    if mode == "moba":
        pos0 = pk_ref[0, :, 0:1]
        @pl.when(qi == 0)
        def _():
            for c in range(0, seq, chunk):
                pk = (pk_ref[0, :, c:c + chunk] - pos0).astype(F32)
                for h in range(2):
                    col = pk * (sl_ref[0, h:h + 1, 0:1] * LOG2E)
                    kaug_sc[h, c:c + chunk, :] = key_side(h, col, c)
            nbp = km_sc.shape[1]
            blk = lax.broadcasted_iota(jnp.int32, (nbp, seq), 0)
            key = lax.broadcasted_iota(jnp.int32, (nbp, seq), 1)
            pool = jnp.where((key >= blk * tk) & (key < (blk + 1) * tk), 1.0, 0.0).astype(BF16)
            km = _dot(pool, k_ref[0]) * (1.0 / tk)
            hi = km.astype(BF16)
            km_sc[0] = hi
            km_sc[1] = (km - hi.astype(F32)).astype(BF16)

    blk0 = qi * sub
    chains = [(i, h) for i in range(sub) for h in range(2)]

    def selection_bias(i, q_masked):
        nbp = km_sc.shape[1]
        n_iota = lax.broadcasted_iota(jnp.int32, (nbp, tq), 0)
        g = _nt_dot(km_sc[0], q_masked) + _nt_dot(km_sc[1], q_masked)
        g = jnp.where(n_iota < blk0 + i, g, -jnp.inf)
        bias = jnp.where(n_iota == blk0 + i, 0.0, MASKED)
        for _ in range(n_sel):
            mx = jnp.max(g, axis=0, keepdims=True)
            idx = jnp.min(jnp.where(g == mx, n_iota, nbp), axis=0, keepdims=True)
            hit = n_iota == idx
            ok = (mx > -jnp.inf) & (mx < jnp.inf)
            bias = jnp.where(hit & ok, 0.0, bias)
            g = jnp.where(hit, -jnp.inf, g)
        return bias

    def query_side(h, q_masked, row_term, sel):
        aug = spare_slab(h, row_term, False, sel)
        return jnp.where(own_lanes(h, aug.shape), q_masked, aug).astype(BF16)

    qh = {}
    for i in range(sub):
        qb = q_ref[0, i * tq:(i + 1) * tq, :]
        if mode == "mla":
            qh[i, 0], qh[i, 1] = qb[:, :LANES], qb[:, LANES:]
            continue
        qf = qb.astype(F32)
        for h in range(2):
            q_masked = jnp.where(own_lanes(h, qf.shape), qf, 0.0)
            if mode == "mem":
                qh[i, h] = q_masked.astype(BF16)
            elif mode == "fox":
                row_term = cq_ref[0, 0, h:h + 1, i * tq:(i + 1) * tq] * LOG2E
                qh[i, h] = query_side(h, q_masked, row_term, None)
            else:
                pq = (pq_ref[0, :, i * tq:(i + 1) * tq] - pos0).astype(F32)
                row_term = -((sl_ref[0, h:h + 1, 0:1] * LOG2E) * pq)
                sel = selection_bias(i, q_masked.astype(BF16)) if n_sel > 0 else None
                qh[i, h] = query_side(h, q_masked, row_term, sel)

    def logits(step):
        i, h, j, own = step
        start = j * tk if isinstance(j, int) else pl.multiple_of(j * tk, tk)
        if mode == "mla":
            kb = k_ref[0, pl.ds(start, tk), h * LANES:(h + 1) * LANES]
        elif mode == "mem":
            kb = k_ref[0, pl.ds(start, tk), :]
        else:
            kb = kaug_sc[h, pl.ds(start, tk), :]
        z = _nt_dot(kb, qh[i, h])
        if own and mode != "mem":
            key = lax.broadcasted_iota(jnp.int32, (tk, tq), 0)
            qry = lax.broadcasted_iota(jnp.int32, (tk, tq), 1)
            z = jnp.where(key <= qry, z, MASKED)
        return z

    ones_rows = jnp.ones((DENOM_ROWS, tk), BF16)

    def fold(step, z):
        i, h, j, own = step
        vt = jnp.concatenate([vt_ref[0, j, h * HEAD_DIM:(h + 1) * HEAD_DIM, :], ones_rows], axis=0)
        mz = jnp.max(z, axis=0, keepdims=True)
        if own:
            m_sc[i, h] = mz
            acc_sc[i, h] = _dot(vt, jnp.exp2(z - mz).astype(BF16))
        else:
            m_prev = m_sc[i, h]
            m_new = jnp.maximum(m_prev, mz)
            alpha = jnp.exp2(m_prev - m_new)
            acc_sc[i, h] = alpha * acc_sc[i, h] + _dot(vt, jnp.exp2(z - m_new).astype(BF16))
            m_sc[i, h] = m_new

    def run(steps):
        zs = {s: logits(steps[s]) for s in range(min(ATTN_LOOKAHEAD, len(steps)))}
        for s in range(len(steps)):
            if s + ATTN_LOOKAHEAD < len(steps):
                zs[s + ATTN_LOOKAHEAD] = logits(steps[s + ATTN_LOOKAHEAD])
            fold(steps[s], zs.pop(s))

    if mode == "mem":
        run([(i, h, 0, True) for i, h in chains])
    else:
        steps = [(i, h, blk0 + i, True) for i, h in chains]
        steps += [(i, h, blk0 + b, False) for b in range(sub - 1) for i, h in chains if i > b]
        run(steps)

        def body(g, carry):
            run([(i, h, g * sub + b, False) for b in range(sub) for i, h in chains])
            return carry
        lax.fori_loop(0, qi, body, 0)

    for i in range(sub):
        out_t = jnp.concatenate(
            [acc_sc[i, h, :HEAD_DIM, :] / acc_sc[i, h, HEAD_DIM:HEAD_DIM + 1, :] for h in range(2)],
            axis=0)
        o_ref[0, i * tq:(i + 1) * tq, :] = out_t.T.astype(BF16)


def _attention(mode, q, k, vt, extra=(), n_sel=0):
    b, s, _ = q.shape
    sk = k.shape[1]
    n_pairs = vt.shape[2] // LANES
    sub = ATTN_SUB
    tq = Q_TILE
    ts = sub * tq
    assert s % ts == 0
    qw = 2 * LANES if mode == "mla" else LANES
    in_specs = [pl.BlockSpec((1, ts, qw), lambda bi, p, i: (bi, i, p)),
                pl.BlockSpec((1, sk, qw), lambda bi, p, i: (bi, 0, p)),
                pl.BlockSpec((1, sk // KV_BLOCK, LANES, KV_BLOCK), lambda bi, p, i: (bi, 0, p, 0))]
    scratch = [pltpu.VMEM((sub, 2, 1, tq), F32),
               pltpu.VMEM((sub, 2, HEAD_DIM + DENOM_ROWS, tq), F32)]
    if mode == "fox":
        in_specs += [pl.BlockSpec((1, 1, 2, ts), lambda bi, p, i: (bi, p, 0, i)),
                     pl.BlockSpec((1, 1, 2, s), lambda bi, p, i: (bi, p, 0, 0))]
        scratch += [pltpu.VMEM((2, s, LANES), BF16)]
    elif mode == "moba":
        nbp = -(-(s // KV_BLOCK) // 16) * 16
        in_specs += [pl.BlockSpec((1, 1, ts), lambda bi, p, i: (bi, 0, i)),
                     pl.BlockSpec((1, 1, s), lambda bi, p, i: (bi, 0, 0)),
                     pl.BlockSpec((1, 2, LANES), lambda bi, p, i: (p, 0, 0))]
        assert SEL_LANE + nbp <= HEAD_DIM
        scratch += [pltpu.VMEM((2, s, LANES), BF16), pltpu.VMEM((2, nbp, LANES), BF16)]
    return pl.pallas_call(
        functools.partial(_attn_kernel, mode=mode, tq=tq, sub=sub, seq=sk, n_sel=n_sel),
        grid=(b, n_pairs, s // ts),
        in_specs=in_specs,
        out_specs=pl.BlockSpec((1, ts, LANES), lambda bi, p, i: (bi, i, p)),
        out_shape=jax.ShapeDtypeStruct((b, s, n_pairs * LANES), BF16),
        scratch_shapes=scratch,
        compiler_params=_params(("parallel", "parallel", "arbitrary")),
        name="attn_" + mode,
    )(q, k, vt, *extra)


def _rope_table_kernel(pos_ref, invf_ref, cos_ref, sin_ref):
    ang = pos_ref[...].astype(F32) * invf_ref[...]
    cos_ref[...] = jnp.cos(ang)
    sin_ref[...] = jnp.sin(ang)


def _rope_tables(positions):
    t = positions.size
    half = MLA_ROPE_DIM // 2
    per_row = LANES // half
    inv_freq = ROPE_THETA ** (-jnp.arange(0, MLA_ROPE_DIM, 2, dtype=F32) / MLA_ROPE_DIM)
    pos = jnp.repeat(positions.reshape(t // per_row, per_row), half, axis=1)
    rows = min(t // per_row, 1024)
    spec = pl.BlockSpec((rows, LANES), lambda i: (i, 0))
    cos, sin = pl.pallas_call(
        _rope_table_kernel, grid=(t // per_row // rows,),
        in_specs=[spec, pl.BlockSpec((1, LANES), lambda i: (0, 0))],
        out_specs=[spec, spec],
        out_shape=[jax.ShapeDtypeStruct((t // per_row, LANES), F32)] * 2,
        compiler_params=_params(("parallel",)),
        name="rope_tables",
    )(pos, jnp.tile(inv_freq, per_row).reshape(1, LANES))
    pad_r = LANES - MLA_NOPE_DIM - MLA_ROPE_DIM
    slab = lambda v, fill: jnp.concatenate(
        [jnp.full((t, MLA_NOPE_DIM), fill, F32), v.reshape(t, half), v.reshape(t, half),
         jnp.full((t, pad_r), fill, F32)], axis=1)
    return slab(cos, 1.0), slab(sin, 0.0)


def _mla_proj_kernel(x_ref, cos_ref, sin_ref, g_ref, w1_ref, qa_ref, kva_ref, wq_ref, wqr_ref, wk_ref,
                     wvt_ref, qg_ref, kg_ref, mg_ref, q_ref, k_ref, vt_ref, qm_ref, *, tm):
    h = (_rms(x_ref[...]) * g_ref[...]).astype(BF16)
    p = _dot(h, w1_ref[...])
    o = MLA_Q_RANK
    q_lat = p[:, :o]
    kv_lat = p[:, o:o + LANES]
    kr = p[:, o + LANES:o + 2 * LANES]
    kr_rot = p[:, o + 2 * LANES:o + 3 * LANES]
    ym = p[:, o + 3 * LANES:]
    qn = (_rms(q_lat) * qa_ref[...]).astype(BF16)
    kvn = (_rms(kv_lat) * kva_ref[...]).astype(BF16)
    cos, sin = cos_ref[...], sin_ref[...]
    k_rope = kr * cos + kr_rot * sin
    width = N_MIX_HEADS * LANES
    for c in range(0, width, 2 * LANES):
        qc = _dot(qn, wq_ref[:, c:c + 2 * LANES])
        qr = _dot(qn, wqr_ref[:, c:c + 2 * LANES])
        kc = _dot(kvn, wk_ref[:, c:c + 2 * LANES])
        for s in range(2):
            sl = slice(s * LANES, (s + 1) * LANES)
            out = slice(c + s * LANES, c + (s + 1) * LANES)
            qq = qc[:, sl] * cos + qr[:, sl] * sin
            q_ref[:, out] = (_slab_norm(qq, qg_ref[...], MLA_QK_DIM)
                             * (MLA_QK_DIM ** -0.5 * LOG2E)).astype(BF16)
            k_ref[:, out] = _slab_norm(kc[:, sl] + k_rope, kg_ref[...], MLA_QK_DIM).astype(BF16)
    _store_vt(vt_ref, _nt_dot(wvt_ref[...], kvn).astype(BF16), tm)
    for s in range(MEM_WIDTH // LANES):
        qm_ref[:, s * LANES:(s + 1) * LANES] = (
            _pair_norm(ym[:, s * LANES:(s + 1) * LANES], mg_ref[...]) * (HEAD_DIM ** -0.5 * LOG2E)
        ).astype(BF16)


def _rot_half(w):
    half = MLA_ROPE_DIM // 2
    return jnp.concatenate([-w[..., half:], w[..., :half]], axis=-1)


def _mla_proj(x2, cos, sin, seq, g, w_in, qa, kva, w_q_up, w_kv_up, q_gain, k_gain, m_gain):
    t, d = x2.shape
    tm = ROW_TILE
    per_seq = seq // tm
    nh, nope, rope_d = N_MIX_HEADS, MLA_NOPE_DIM, MLA_ROPE_DIM
    pad_r = LANES - nope - rope_d
    o = MLA_Q_RANK + MLA_KV_RANK
    k_r = w_in[:, o:o + rope_d]
    slab = lambda w: jnp.pad(w, ((0, 0), (nope, pad_r)))
    w1 = jnp.concatenate([w_in[:, :o], slab(k_r), slab(_rot_half(k_r)), w_in[:, -MEM_WIDTH:]],
                         axis=1).astype(BF16)
    wq3 = w_q_up.reshape(MLA_Q_RANK, nh, MLA_QK_DIM)
    wq = jnp.pad(wq3, ((0, 0), (0, 0), (0, pad_r))).reshape(MLA_Q_RANK, nh * LANES).astype(BF16)
    wqr = jnp.pad(_rot_half(wq3[..., nope:]), ((0, 0), (0, 0), (nope, pad_r))
                  ).reshape(MLA_Q_RANK, nh * LANES).astype(BF16)
    wkv3 = w_kv_up.reshape(MLA_KV_RANK, nh, nope + MLA_V_DIM)
    wk = jnp.pad(wkv3[..., :nope], ((0, 0), (0, 0), (0, LANES - nope))
                 ).reshape(MLA_KV_RANK, nh * LANES).astype(BF16)
    wvt = wkv3[..., nope:].reshape(MLA_KV_RANK, nh * MLA_V_DIM).T.astype(BF16)
    padg = lambda v: jnp.pad(v, (0, pad_r)).reshape(1, LANES)
    const = lambda *shape: pl.BlockSpec(shape, lambda i: (0,) * len(shape))
    rows = lambda w: pl.BlockSpec((tm, w), lambda i: (i, 0))
    qkw = nh * LANES
    return pl.pallas_call(
        functools.partial(_mla_proj_kernel, tm=tm),
        grid=(t // tm,),
        in_specs=[rows(d), rows(LANES), rows(LANES), const(1, d), const(d, w1.shape[1]),
                  const(1, MLA_Q_RANK), const(1, MLA_KV_RANK), const(MLA_Q_RANK, qkw),
                  const(MLA_Q_RANK, qkw), const(MLA_KV_RANK, qkw), const(MIX_WIDTH, MLA_KV_RANK),
                  const(1, LANES), const(1, LANES), const(1, LANES)],
        out_specs=[rows(qkw), rows(qkw),
                   pl.BlockSpec((1, tm // KV_BLOCK, MIX_WIDTH, KV_BLOCK),
                                lambda i: (i // per_seq, i % per_seq, 0, 0)),
                   rows(MEM_WIDTH)],
        out_shape=[jax.ShapeDtypeStruct((t, qkw), BF16), jax.ShapeDtypeStruct((t, qkw), BF16),
                   jax.ShapeDtypeStruct((t // seq, seq // KV_BLOCK, MIX_WIDTH, KV_BLOCK), BF16),
                   jax.ShapeDtypeStruct((t, MEM_WIDTH), BF16)],
        compiler_params=_params(("parallel",)),
        name="mla_proj",
    )(x2, cos, sin, g.reshape(1, d), w1, qa.reshape(1, -1), kva.reshape(1, -1), wq, wqr, wk, wvt,
      padg(q_gain), padg(k_gain), jnp.tile(m_gain, 2).reshape(1, LANES))


def _ffn_kernel(x_ref, xh_ref, om_ref, omh_ref, oc_ref, och_ref, wo1_ref, wo2_ref, g_ref, wup_ref,
                cw_ref, cb_ref, wdn_ref, o_ref, u_sc, act_sc, *, tm, per_seq):
    i = pl.program_id(0)
    x = x_ref[...] + _dot(om_ref[...], wo1_ref[...]) + _dot(oc_ref[...], wo2_ref[...])
    x_halo = xh_ref[...] + _dot(omh_ref[...], wo1_ref[...]) + _dot(och_ref[...], wo2_ref[...])
    h = (_rms(x) * g_ref[...]).astype(BF16)
    hh = (_rms(x_halo) * g_ref[...]).astype(BF16)
    keep = jnp.where(i % per_seq == 0, 0.0, 1.0)
    tf = FF_CHUNK
    n_chunks = D_FF // tf

    def columns(c, part):
        return slice(part * D_FF + c * tf, part * D_FF + (c + 1) * tf)

    def up(c):
        for part in range(2):
            w = wup_ref[:, columns(c, part)]
            u_sc[c % 2, part, 0:HALO, :] = _dot(hh, w) * keep
            u_sc[c % 2, part, HALO:HALO + tm, :] = _dot(h, w)

    def activate(c):
        halves = []
        for part in range(2):
            cw = cw_ref[:, columns(c, part)]
            u = u_sc.at[c % 2, part]
            halves.append(cw[2:3] * u[HALO:HALO + tm, :] + cw[1:2] * u[HALO - 1:HALO - 1 + tm, :]
                          + cw[0:1] * u[HALO - 2:HALO - 2 + tm, :] + cb_ref[:, columns(c, part)])
        gate, val = halves
        act_sc[:, c * tf:(c + 1) * tf] = (gate * jax.nn.sigmoid(gate) * val).astype(BF16)

    up(0)
    for c in range(n_chunks):
        if c + 1 < n_chunks:
            up(c + 1)
        activate(c)
    o_ref[...] = x + _dot(act_sc[...], wdn_ref[...])


def _out_proj_ffn(x2, o_mix, o_mem, seq, w_out, g, w_up, conv_w, conv_b, w_down):
    t, d = x2.shape
    tm = ROW_TILE
    per_seq = seq // tm
    rows = lambda w: pl.BlockSpec((tm, w), lambda i: (i, 0))
    halo = lambda w: pl.BlockSpec((HALO, w), lambda i: (jnp.maximum(i * (tm // HALO) - 1, 0), 0))
    resident = lambda *shape: pl.BlockSpec(shape, lambda i: (0,) * len(shape),
                                           pipeline_mode=pl.Buffered(1))
    return pl.pallas_call(
        functools.partial(_ffn_kernel, tm=tm, per_seq=per_seq),
        grid=(t // tm,),
        in_specs=[rows(d), halo(d), rows(MIX_WIDTH), halo(MIX_WIDTH), rows(MEM_WIDTH), halo(MEM_WIDTH),
                  resident(MIX_WIDTH, d), resident(MEM_WIDTH, d), resident(1, d),
                  resident(d, 2 * D_FF), resident(3, 2 * D_FF), resident(1, 2 * D_FF),
                  resident(D_FF, d)],
        out_specs=rows(d), out_shape=jax.ShapeDtypeStruct((t, d), F32),
        scratch_shapes=[pltpu.VMEM((2, 2, HALO + tm, FF_CHUNK), F32), pltpu.VMEM((tm, D_FF), BF16)],
        compiler_params=_params(("parallel",)),
        name="out_proj_conv_ffn",
    )(x2, x2, o_mix, o_mix, o_mem, o_mem, w_out[:MIX_WIDTH].astype(BF16),
      w_out[MIX_WIDTH:].astype(BF16), g.reshape(1, d), w_up.astype(BF16), conv_w,
      conv_b.reshape(1, -1), w_down.astype(BF16))


def kernel(x, mem, positions, norm_mix, norm_ffn, w_out, mem_norm, w_mem_kv, mem_q_gain,
           mem_k_gain, fox_w_in, fox_b_f, fox_q_gain, fox_k_gain, mla_w_in, mla_qa_norm,
           mla_kva_norm, mla_w_q_up, mla_w_kv_up, mla_q_gain, mla_k_gain, moba_w_in,
           moba_q_gain, moba_k_gain, ffn_w_up, ffn_conv_w, ffn_conv_b, ffn_w_down):
    b, s, d = x.shape
    depth = norm_mix.shape[0]
    assert d == D_MODEL and s % ROW_TILE == 0 and mem.shape[1] == KV_BLOCK
    n_pairs = N_MIX_HEADS // 2
    k_mem, vt_mem = _mem_prep(mem, mem_norm, w_mem_kv, mem_k_gain)
    x2 = x.reshape(b * s, d)
    rope_cos, rope_sin = _rope_tables(positions) if depth > 1 else (None, None)
    n_blocks = s // MOBA_BLOCK
    slopes = 2.0 ** (-8.0 * jnp.arange(1, N_MIX_HEADS + 1, dtype=F32) / N_MIX_HEADS)
    slopes = jnp.broadcast_to(slopes.reshape(n_pairs, 2, 1), (n_pairs, 2, LANES))

    for i in range(depth):
        kind, j = i % N_MIXERS, i // N_MIXERS
        if kind == 0:
            q, k, vt, qm, flog = _qkv_proj(x2, s, norm_mix[i], fox_w_in[j], fox_q_gain[j],
                                           fox_k_gain[j], mem_q_gain[i], fox_b_f[j])
            cum = _cumsum(flog.reshape(b, s, LANES))[..., :N_MIX_HEADS]
            cum = cum.transpose(0, 2, 1).reshape(b, n_pairs, 2, s)
            o_mix = _attention("fox", q.reshape(b, s, -1), k.reshape(b, s, -1), vt, (cum, cum))
        elif kind == 1:
            q, k, vt, qm = _mla_proj(x2, rope_cos, rope_sin, s, norm_mix[i], mla_w_in[j], mla_qa_norm[j],
                                     mla_kva_norm[j], mla_w_q_up[j], mla_w_kv_up[j],
                                     mla_q_gain[j], mla_k_gain[j], mem_q_gain[i])
            o_mix = _attention("mla", q.reshape(b, s, -1), k.reshape(b, s, -1), vt)
        else:
            q, k, vt, qm = _qkv_proj(x2, s, norm_mix[i], moba_w_in[j], moba_q_gain[j],
                                     moba_k_gain[j], mem_q_gain[i])
            extra = (positions.reshape(b, 1, s), positions.reshape(b, 1, s), slopes)
            o_mix = _attention("moba", q.reshape(b, s, -1), k.reshape(b, s, -1), vt, extra,
                               n_sel=min(MOBA_TOPK, n_blocks - 1))
        o_mem = _attention("mem", qm.reshape(b, s, MEM_WIDTH), k_mem[i], vt_mem)
        x2 = _out_proj_ffn(x2, o_mix.reshape(b * s, MIX_WIDTH), o_mem.reshape(b * s, MEM_WIDTH), s,
                           w_out[i], norm_ffn[i], ffn_w_up[i], ffn_conv_w[i], ffn_conv_b[i],
                           ffn_w_down[i])
    return x2.reshape(b, s, d)
```

```python
import functools

import jax
import jax.numpy as jnp
from jax import lax
from jax.experimental import pallas as pl
from jax.experimental.pallas import tpu as pltpu

F32 = jnp.float32
BF16 = jnp.bfloat16

D_MODEL = 1024
HEAD_DIM = 64
N_MIX_HEADS = 12
MIX_WIDTH = N_MIX_HEADS * HEAD_DIM
N_MEM_HEADS = 4
MEM_WIDTH = N_MEM_HEADS * HEAD_DIM
N_MIXERS = 3
MLA_Q_RANK = 256
MLA_KV_RANK = 128
MLA_NOPE_DIM = 64
MLA_ROPE_DIM = 32
MLA_QK_DIM = MLA_NOPE_DIM + MLA_ROPE_DIM
MLA_V_DIM = 64
ROPE_THETA = 10000.0
MOBA_BLOCK = 256
MOBA_TOPK = 3
D_FF = 2816
EPS = 1e-6

LANES = 128
KV_BLOCK = 256
Q_TILE = 256
ATTN_SUB = 8
ATTN_LOOKAHEAD = 5
ROW_TILE = 512
FF_CHUNK = 256
HALO = 16
MASKED = -1e30
LOG2E = 1.4426950408889634
SEL_LANE = 8
DENOM_ROWS = 16
VMEM_LIMIT = 56 * 1024 * 1024


def _params(semantics):
    return pltpu.CompilerParams(dimension_semantics=semantics, vmem_limit_bytes=VMEM_LIMIT)


def _nt_dot(a, b):
    return lax.dot_general(a, b, (((1,), (1,)), ((), ())), preferred_element_type=F32)


def _dot(a, b):
    return jnp.dot(a, b, preferred_element_type=F32)


def _rms(x):
    return x * lax.rsqrt(jnp.mean(x * x, axis=-1, keepdims=True) + EPS)


def _pair_norm(y, gain):
    lane = lax.broadcasted_iota(jnp.int32, y.shape, 1)
    lo = lane < HEAD_DIM
    y2 = y * y
    s_lo = jnp.sum(jnp.where(lo, y2, 0.0), axis=-1, keepdims=True)
    s_hi = jnp.sum(jnp.where(lo, 0.0, y2), axis=-1, keepdims=True)
    ms = jnp.where(lo, s_lo, s_hi) * (1.0 / HEAD_DIM)
    return y * lax.rsqrt(ms + EPS) * gain


def _slab_norm(y, gain, width):
    ms = jnp.sum(y * y, axis=-1, keepdims=True) * (1.0 / width)
    return y * lax.rsqrt(ms + EPS) * gain


def _split3(x):
    x1 = x.astype(BF16).astype(F32)
    r = x - x1
    x2 = r.astype(BF16).astype(F32)
    return x1, x2, (r - x2).astype(BF16).astype(F32)


def _store_vt(vt_ref, vt, tm):
    for j in range(tm // KV_BLOCK):
        vt_ref[0, j] = vt[:, j * KV_BLOCK:(j + 1) * KV_BLOCK]


def _mem_kernel(mem_ref, g_ref, wk_ref, wvt_ref, kg_ref, kn_ref, vt_ref, *, depth):
    h = (_rms(mem_ref[0]) * g_ref[...]).astype(BF16)
    k = _dot(h, wk_ref[...])
    for s in range(MEM_WIDTH // LANES):
        slab = k[:, s * LANES:(s + 1) * LANES]
        for i in range(depth):
            kn_ref[i, 0, :, s * LANES:(s + 1) * LANES] = _pair_norm(slab, kg_ref[i]).astype(BF16)
    vt_ref[0, 0] = _nt_dot(wvt_ref[...], h).astype(BF16)


def _mem_prep(mem, mem_norm, w_mem_kv, mem_k_gain):
    b, m, d = mem.shape
    depth = mem_k_gain.shape[0]
    wk = w_mem_kv[:, :MEM_WIDTH].astype(BF16)
    wvt = w_mem_kv[:, MEM_WIDTH:].T.astype(BF16)
    kg = jnp.tile(mem_k_gain, (1, 2)).reshape(depth, 1, LANES)
    const = lambda *shape: pl.BlockSpec(shape, lambda i: (0,) * len(shape))
    return pl.pallas_call(
        functools.partial(_mem_kernel, depth=depth),
        grid=(b,),
        in_specs=[pl.BlockSpec((1, m, d), lambda i: (i, 0, 0)), const(1, d), const(d, MEM_WIDTH),
                  const(MEM_WIDTH, d), const(depth, 1, LANES)],
        out_specs=[pl.BlockSpec((depth, 1, m, MEM_WIDTH), lambda i: (0, i, 0, 0)),
                   pl.BlockSpec((1, 1, MEM_WIDTH, m), lambda i: (i, 0, 0, 0))],
        out_shape=[jax.ShapeDtypeStruct((depth, b, m, MEM_WIDTH), BF16),
                   jax.ShapeDtypeStruct((b, 1, MEM_WIDTH, m), BF16)],
        compiler_params=_params(("parallel",)),
        name="mem_prep",
    )(mem, mem_norm.reshape(1, d), wk, wvt, kg)


def _qkv_proj_kernel(x_ref, g_ref, wqk_ref, wvt_ref, wm_ref, qg_ref, kg_ref, mg_ref, *rest,
                     has_gate, tm):
    if has_gate:
        wf_ref, bf_ref, q_ref, k_ref, vt_ref, qm_ref, f_ref = rest
    else:
        q_ref, k_ref, vt_ref, qm_ref = rest
    h = (_rms(x_ref[...]) * g_ref[...]).astype(BF16)
    for c in range(0, 2 * MIX_WIDTH, 2 * LANES):
        y = _dot(h, wqk_ref[:, c:c + 2 * LANES])
        for s in range(2):
            col = c + s * LANES
            slab = y[:, s * LANES:(s + 1) * LANES]
            if col < MIX_WIDTH:
                q_ref[:, col:col + LANES] = (
                    _pair_norm(slab, qg_ref[...]) * (HEAD_DIM ** -0.5 * LOG2E)).astype(BF16)
            else:
                k_ref[:, col - MIX_WIDTH:col - MIX_WIDTH + LANES] = (
                    _pair_norm(slab, kg_ref[...]).astype(BF16))
    _store_vt(vt_ref, _nt_dot(wvt_ref[...], h).astype(BF16), tm)
    ym = _dot(h, wm_ref[...])
    for s in range(MEM_WIDTH // LANES):
        qm_ref[:, s * LANES:(s + 1) * LANES] = (
            _pair_norm(ym[:, s * LANES:(s + 1) * LANES], mg_ref[...]) * (HEAD_DIM ** -0.5 * LOG2E)
        ).astype(BF16)
    if has_gate:
        f = _dot(h, wf_ref[...]) + bf_ref[...]
        f_ref[...] = jnp.minimum(f, 0.0) - jnp.log1p(jnp.exp(-jnp.abs(f)))


def _qkv_proj(x2, seq, g, w_in, q_gain, k_gain, m_gain, b_f=None):
    t, d = x2.shape
    tm = ROW_TILE
    nblk = seq // KV_BLOCK
    per_seq = seq // tm
    has_gate = b_f is not None
    wqk = w_in[:, :2 * MIX_WIDTH].astype(BF16)
    wvt = w_in[:, 2 * MIX_WIDTH:3 * MIX_WIDTH].T.astype(BF16)
    wm = w_in[:, -MEM_WIDTH:].astype(BF16)
    pair = lambda v: jnp.tile(v, 2).reshape(1, LANES)
    const = lambda *shape: pl.BlockSpec(shape, lambda i: (0,) * len(shape))
    rows = lambda w: pl.BlockSpec((tm, w), lambda i: (i, 0))
    args = [x2, g.reshape(1, d), wqk, wvt, wm, pair(q_gain), pair(k_gain), pair(m_gain)]
    in_specs = [rows(d), const(1, d), const(d, 2 * MIX_WIDTH), const(MIX_WIDTH, d),
                const(d, MEM_WIDTH), const(1, LANES), const(1, LANES), const(1, LANES)]
    out_specs = [rows(MIX_WIDTH), rows(MIX_WIDTH),
                 pl.BlockSpec((1, tm // KV_BLOCK, MIX_WIDTH, KV_BLOCK),
                              lambda i: (i // per_seq, i % per_seq, 0, 0)),
                 rows(MEM_WIDTH)]
    out_shape = [jax.ShapeDtypeStruct((t, MIX_WIDTH), BF16), jax.ShapeDtypeStruct((t, MIX_WIDTH), BF16),
                 jax.ShapeDtypeStruct((t // seq, nblk, MIX_WIDTH, KV_BLOCK), BF16),
                 jax.ShapeDtypeStruct((t, MEM_WIDTH), BF16)]
    if has_gate:
        nh = N_MIX_HEADS
        wf = jnp.pad(w_in[:, 3 * MIX_WIDTH:3 * MIX_WIDTH + nh], ((0, 0), (0, LANES - nh))).astype(BF16)
        args += [wf, jnp.pad(b_f, (0, LANES - nh)).reshape(1, LANES)]
        in_specs += [const(d, LANES), const(1, LANES)]
        out_specs.append(rows(LANES))
        out_shape.append(jax.ShapeDtypeStruct((t, LANES), F32))
    return pl.pallas_call(
        functools.partial(_qkv_proj_kernel, has_gate=has_gate, tm=tm),
        grid=(t // tm,), in_specs=in_specs, out_specs=out_specs, out_shape=out_shape,
        compiler_params=_params(("parallel",)),
        name="qkv_proj_gate" if has_gate else "qkv_proj",
    )(*args)


def _cumsum_kernel(f_ref, o_ref, *, seq):
    r = lax.broadcasted_iota(jnp.int32, (KV_BLOCK, KV_BLOCK), 0)
    c = lax.broadcasted_iota(jnp.int32, (KV_BLOCK, KV_BLOCK), 1)
    tri = jnp.where(c <= r, 1.0, 0.0).astype(BF16)
    carry = jnp.zeros((1, LANES), F32)
    for j in range(seq // KV_BLOCK):
        x = f_ref[0, j * KV_BLOCK:(j + 1) * KV_BLOCK, :]
        x1 = x.astype(BF16)
        r1 = x - x1.astype(F32)
        x2 = r1.astype(BF16)
        x3 = (r1 - x2.astype(F32)).astype(BF16)
        cs = _dot(tri, x1) + _dot(tri, x2) + _dot(tri, x3) + carry
        o_ref[0, j * KV_BLOCK:(j + 1) * KV_BLOCK, :] = cs
        carry = cs[KV_BLOCK - 1:KV_BLOCK, :]


def _cumsum(flog):
    b, seq, w = flog.shape
    spec = pl.BlockSpec((1, seq, w), lambda i: (i, 0, 0))
    return pl.pallas_call(
        functools.partial(_cumsum_kernel, seq=seq),
        grid=(b,), in_specs=[spec], out_specs=spec,
        out_shape=jax.ShapeDtypeStruct((b, seq, w), F32),
        compiler_params=_params(("parallel",)),
        name="gate_cumsum",
    )(flog)


def _attn_kernel(*refs, mode, tq, sub, seq, n_sel):
    if mode == "fox":
        q_ref, k_ref, vt_ref, cq_ref, ck_ref, o_ref, m_sc, acc_sc, kaug_sc = refs
    elif mode == "moba":
        (q_ref, k_ref, vt_ref, pq_ref, pk_ref, sl_ref, o_ref,
         m_sc, acc_sc, kaug_sc, km_sc) = refs
    else:
        q_ref, k_ref, vt_ref, o_ref, m_sc, acc_sc = refs
    qi = pl.program_id(2)
    tk = KV_BLOCK
    nblk = seq // tk

    def spare(h):
        return HEAD_DIM * (1 - h)

    def own_lanes(h, shape):
        lane = lax.broadcasted_iota(jnp.int32, shape, 1)
        return (lane >= HEAD_DIM * h) & (lane < HEAD_DIM * (h + 1))

    def spare_slab(h, term, term_first, extra):
        n = term.shape[1]
        t1, t2, t3 = _split3(term)
        sl8 = lax.broadcasted_iota(jnp.int32, (8, n), 0)
        t_at, one_at = (0, 3) if term_first else (3, 0)
        head = jnp.where(sl8 == t_at, t1, jnp.where(sl8 == t_at + 1, t2, jnp.where(
            sl8 == t_at + 2, t3, jnp.where((sl8 >= one_at) & (sl8 < one_at + 3), 1.0, 0.0))))
        pieces = [head] if extra is None else [head, extra]
        used = sum(p.shape[0] for p in pieces)
        if spare(h):
            pieces.insert(0, jnp.zeros((spare(h), n), F32))
        pieces.append(jnp.zeros((LANES - spare(h) - used, n), F32))
        return jnp.concatenate(pieces, axis=0).T

    def key_side(h, col, first_key):
        n = col.shape[1]
        extra = None
        if mode == "moba":
            nbp = km_sc.shape[1]
            key = lax.broadcasted_iota(jnp.int32, (nbp, n), 1) + first_key
            blk = lax.shift_right_logical(key, tk.bit_length() - 1)
            extra = jnp.where(blk == lax.broadcasted_iota(jnp.int32, (nbp, n), 0), 1.0, 0.0)
        aug = spare_slab(h, col, True, extra)
        k = k_ref[0, first_key:first_key + n, :].astype(F32)
        return jnp.where(own_lanes(h, k.shape), k, aug).astype(BF16)

    chunk = 4 * tk
    if mode == "fox":
        @pl.when(qi == 0)
        def _():
            for h in range(2):
                for c in range(0, seq, chunk):
                    ck = ck_ref[0, 0, h:h + 1, c:c + chunk] * -LOG2E
                    kaug_sc[h, c:c + chunk, :] = key_side(h, ck, c)
    if mode == "moba":
        pos0 = pk_ref[0, :, 0:1]
        @pl.when(qi == 0)
        def _():
            for c in range(0, seq, chunk):
                pk = (pk_ref[0, :, c:c + chunk] - pos0).astype(F32)
                for h in range(2):
                    col = pk * (sl_ref[0, h:h + 1, 0:1] * LOG2E)
                    kaug_sc[h, c:c + chunk, :] = key_side(h, col, c)
            nbp = km_sc.shape[1]
            blk = lax.broadcasted_iota(jnp.int32, (nbp, seq), 0)
            key = lax.broadcasted_iota(jnp.int32, (nbp, seq), 1)
            pool = jnp.where((key >= blk * tk) & (key < (blk + 1) * tk), 1.0, 0.0).astype(BF16)
            km = _dot(pool, k_ref[0]) * (1.0 / tk)
            hi = km.astype(BF16)
            km_sc[0] = hi
            km_sc[1] = (km - hi.astype(F32)).astype(BF16)

    blk0 = qi * sub
    chains = [(i, h) for i in range(sub) for h in range(2)]

    def selection_bias(i, q_masked):
        nbp = km_sc.shape[1]
        n_iota = lax.broadcasted_iota(jnp.int32, (nbp, tq), 0)
        g = _nt_dot(km_sc[0], q_masked) + _nt_dot(km_sc[1], q_masked)
        g = jnp.where(n_iota < blk0 + i, g, -jnp.inf)
        bias = jnp.where(n_iota == blk0 + i, 0.0, MASKED)
        for _ in range(n_sel):
            mx = jnp.max(g, axis=0, keepdims=True)
            idx = jnp.min(jnp.where(g == mx, n_iota, nbp), axis=0, keepdims=True)
            hit = n_iota == idx
            ok = (mx > -jnp.inf) & (mx < jnp.inf)
            bias = jnp.where(hit & ok, 0.0, bias)
            g = jnp.where(hit, -jnp.inf, g)
        return bias

    def query_side(h, q_masked, row_term, sel):
        aug = spare_slab(h, row_term, False, sel)
        return jnp.where(own_lanes(h, aug.shape), q_masked, aug).astype(BF16)

    qh = {}
    for i in range(sub):
        qb = q_ref[0, i * tq:(i + 1) * tq, :]
        if mode == "mla":
            qh[i, 0], qh[i, 1] = qb[:, :LANES], qb[:, LANES:]
            continue
        qf = qb.astype(F32)
        for h in range(2):
            q_masked = jnp.where(own_lanes(h, qf.shape), qf, 0.0)
            if mode == "mem":
                qh[i, h] = q_masked.astype(BF16)
            elif mode == "fox":
                row_term = cq_ref[0, 0, h:h + 1, i * tq:(i + 1) * tq] * LOG2E
                qh[i, h] = query_side(h, q_masked, row_term, None)
            else:
                pq = (pq_ref[0, :, i * tq:(i + 1) * tq] - pos0).astype(F32)
                row_term = -((sl_ref[0, h:h + 1, 0:1] * LOG2E) * pq)
                sel = selection_bias(i, q_masked.astype(BF16)) if n_sel > 0 else None
                qh[i, h] = query_side(h, q_masked, row_term, sel)

    def logits(step):
        i, h, j, own = step
        start = j * tk if isinstance(j, int) else pl.multiple_of(j * tk, tk)
        if mode == "mla":
            kb = k_ref[0, pl.ds(start, tk), h * LANES:(h + 1) * LANES]
        elif mode == "mem":
            kb = k_ref[0, pl.ds(start, tk), :]
        else:
            kb = kaug_sc[h, pl.ds(start, tk), :]
        z = _nt_dot(kb, qh[i, h])
        if own and mode != "mem":
            key = lax.broadcasted_iota(jnp.int32, (tk, tq), 0)
            qry = lax.broadcasted_iota(jnp.int32, (tk, tq), 1)
            z = jnp.where(key <= qry, z, MASKED)
        return z

    ones_rows = jnp.ones((DENOM_ROWS, tk), BF16)

    def fold(step, z):
        i, h, j, own = step
        vt = jnp.concatenate([vt_ref[0, j, h * HEAD_DIM:(h + 1) * HEAD_DIM, :], ones_rows], axis=0)
        mz = jnp.max(z, axis=0, keepdims=True)
        if own:
            m_sc[i, h] = mz
            acc_sc[i, h] = _dot(vt, jnp.exp2(z - mz).astype(BF16))
        else:
            m_prev = m_sc[i, h]
            m_new = jnp.maximum(m_prev, mz)
            alpha = jnp.exp2(m_prev - m_new)
            acc_sc[i, h] = alpha * acc_sc[i, h] + _dot(vt, jnp.exp2(z - m_new).astype(BF16))
            m_sc[i, h] = m_new

    def run(steps):
        zs = {s: logits(steps[s]) for s in range(min(ATTN_LOOKAHEAD, len(steps)))}
        for s in range(len(steps)):
            if s + ATTN_LOOKAHEAD < len(steps):
                zs[s + ATTN_LOOKAHEAD] = logits(steps[s + ATTN_LOOKAHEAD])
            fold(steps[s], zs.pop(s))

    if mode == "mem":
        run([(i, h, 0, True) for i, h in chains])
    else:
        steps = [(i, h, blk0 + i, True) for i, h in chains]
        steps += [(i, h, blk0 + b, False) for b in range(sub - 1) for i, h in chains if i > b]
        run(steps)

        def body(g, carry):
            run([(i, h, g * sub + b, False) for b in range(sub) for i, h in chains])
            return carry
        lax.fori_loop(0, qi, body, 0)

    for i in range(sub):
        out_t = jnp.concatenate(
            [acc_sc[i, h, :HEAD_DIM, :] / acc_sc[i, h, HEAD_DIM:HEAD_DIM + 1, :] for h in range(2)],
            axis=0)
        o_ref[0, i * tq:(i + 1) * tq, :] = out_t.T.astype(BF16)


def _attention(mode, q, k, vt, extra=(), n_sel=0):
    b, s, _ = q.shape
    sk = k.shape[1]
    n_pairs = vt.shape[2] // LANES
    sub = ATTN_SUB
    tq = Q_TILE
    ts = sub * tq
    assert s % ts == 0
    qw = 2 * LANES if mode == "mla" else LANES
    in_specs = [pl.BlockSpec((1, ts, qw), lambda bi, p, i: (bi, i, p)),
                pl.BlockSpec((1, sk, qw), lambda bi, p, i: (bi, 0, p)),
                pl.BlockSpec((1, sk // KV_BLOCK, LANES, KV_BLOCK), lambda bi, p, i: (bi, 0, p, 0))]
    scratch = [pltpu.VMEM((sub, 2, 1, tq), F32),
               pltpu.VMEM((sub, 2, HEAD_DIM + DENOM_ROWS, tq), F32)]
    if mode == "fox":
        in_specs += [pl.BlockSpec((1, 1, 2, ts), lambda bi, p, i: (bi, p, 0, i)),
                     pl.BlockSpec((1, 1, 2, s), lambda bi, p, i: (bi, p, 0, 0))]
        scratch += [pltpu.VMEM((2, s, LANES), BF16)]
    elif mode == "moba":
        nbp = -(-(s // KV_BLOCK) // 16) * 16
        in_specs += [pl.BlockSpec((1, 1, ts), lambda bi, p, i: (bi, 0, i)),
                     pl.BlockSpec((1, 1, s), lambda bi, p, i: (bi, 0, 0)),
                     pl.BlockSpec((1, 2, LANES), lambda bi, p, i: (p, 0, 0))]
        assert SEL_LANE + nbp <= HEAD_DIM
        scratch += [pltpu.VMEM((2, s, LANES), BF16), pltpu.VMEM((2, nbp, LANES), BF16)]
    return pl.pallas_call(
        functools.partial(_attn_kernel, mode=mode, tq=tq, sub=sub, seq=sk, n_sel=n_sel),
        grid=(b, n_pairs, s // ts),
        in_specs=in_specs,
        out_specs=pl.BlockSpec((1, ts, LANES), lambda bi, p, i: (bi, i, p)),
        out_shape=jax.ShapeDtypeStruct((b, s, n_pairs * LANES), BF16),
        scratch_shapes=scratch,
        compiler_params=_params(("parallel", "parallel", "arbitrary")),
        name="attn_" + mode,
    )(q, k, vt, *extra)


def _rope_table_kernel(pos_ref, invf_ref, cos_ref, sin_ref):
    ang = pos_ref[...].astype(F32) * invf_ref[...]
    cos_ref[...] = jnp.cos(ang)
    sin_ref[...] = jnp.sin(ang)


def _rope_tables(positions):
    t = positions.size
    half = MLA_ROPE_DIM // 2
    per_row = LANES // half
    inv_freq = ROPE_THETA ** (-jnp.arange(0, MLA_ROPE_DIM, 2, dtype=F32) / MLA_ROPE_DIM)
    pos = jnp.repeat(positions.reshape(t // per_row, per_row), half, axis=1)
    rows = min(t // per_row, 1024)
    spec = pl.BlockSpec((rows, LANES), lambda i: (i, 0))
    cos, sin = pl.pallas_call(
        _rope_table_kernel, grid=(t // per_row // rows,),
        in_specs=[spec, pl.BlockSpec((1, LANES), lambda i: (0, 0))],
        out_specs=[spec, spec],
        out_shape=[jax.ShapeDtypeStruct((t // per_row, LANES), F32)] * 2,
        compiler_params=_params(("parallel",)),
        name="rope_tables",
    )(pos, jnp.tile(inv_freq, per_row).reshape(1, LANES))
    pad_r = LANES - MLA_NOPE_DIM - MLA_ROPE_DIM
    slab = lambda v, fill: jnp.concatenate(
        [jnp.full((t, MLA_NOPE_DIM), fill, F32), v.reshape(t, half), v.reshape(t, half),
         jnp.full((t, pad_r), fill, F32)], axis=1)
    return slab(cos, 1.0), slab(sin, 0.0)


def _mla_proj_kernel(x_ref, cos_ref, sin_ref, g_ref, w1_ref, qa_ref, kva_ref, wq_ref, wqr_ref, wk_ref,
                     wvt_ref, qg_ref, kg_ref, mg_ref, q_ref, k_ref, vt_ref, qm_ref, *, tm):
    h = (_rms(x_ref[...]) * g_ref[...]).astype(BF16)
    p = _dot(h, w1_ref[...])
    o = MLA_Q_RANK
    q_lat = p[:, :o]
    kv_lat = p[:, o:o + LANES]
    kr = p[:, o + LANES:o + 2 * LANES]
    kr_rot = p[:, o + 2 * LANES:o + 3 * LANES]
    ym = p[:, o + 3 * LANES:]
    qn = (_rms(q_lat) * qa_ref[...]).astype(BF16)
    kvn = (_rms(kv_lat) * kva_ref[...]).astype(BF16)
    cos, sin = cos_ref[...], sin_ref[...]
    k_rope = kr * cos + kr_rot * sin
    width = N_MIX_HEADS * LANES
    for c in range(0, width, 2 * LANES):
        qc = _dot(qn, wq_ref[:, c:c + 2 * LANES])
        qr = _dot(qn, wqr_ref[:, c:c + 2 * LANES])
        kc = _dot(kvn, wk_ref[:, c:c + 2 * LANES])
        for s in range(2):
            sl = slice(s * LANES, (s + 1) * LANES)
            out = slice(c + s * LANES, c + (s + 1) * LANES)
            qq = qc[:, sl] * cos + qr[:, sl] * sin
            q_ref[:, out] = (_slab_norm(qq, qg_ref[...], MLA_QK_DIM)
                             * (MLA_QK_DIM ** -0.5 * LOG2E)).astype(BF16)
            k_ref[:, out] = _slab_norm(kc[:, sl] + k_rope, kg_ref[...], MLA_QK_DIM).astype(BF16)
    _store_vt(vt_ref, _nt_dot(wvt_ref[...], kvn).astype(BF16), tm)
    for s in range(MEM_WIDTH // LANES):
        qm_ref[:, s * LANES:(s + 1) * LANES] = (
            _pair_norm(ym[:, s * LANES:(s + 1) * LANES], mg_ref[...]) * (HEAD_DIM ** -0.5 * LOG2E)
        ).astype(BF16)


def _rot_half(w):
    half = MLA_ROPE_DIM // 2
    return jnp.concatenate([-w[..., half:], w[..., :half]], axis=-1)


def _mla_proj(x2, cos, sin, seq, g, w_in, qa, kva, w_q_up, w_kv_up, q_gain, k_gain, m_gain):
    t, d = x2.shape
    tm = ROW_TILE
    per_seq = seq // tm
    nh, nope, rope_d = N_MIX_HEADS, MLA_NOPE_DIM, MLA_ROPE_DIM
    pad_r = LANES - nope - rope_d
    o = MLA_Q_RANK + MLA_KV_RANK
    k_r = w_in[:, o:o + rope_d]
    slab = lambda w: jnp.pad(w, ((0, 0), (nope, pad_r)))
    w1 = jnp.concatenate([w_in[:, :o], slab(k_r), slab(_rot_half(k_r)), w_in[:, -MEM_WIDTH:]],
                         axis=1).astype(BF16)
    wq3 = w_q_up.reshape(MLA_Q_RANK, nh, MLA_QK_DIM)
    wq = jnp.pad(wq3, ((0, 0), (0, 0), (0, pad_r))).reshape(MLA_Q_RANK, nh * LANES).astype(BF16)
    wqr = jnp.pad(_rot_half(wq3[..., nope:]), ((0, 0), (0, 0), (nope, pad_r))
                  ).reshape(MLA_Q_RANK, nh * LANES).astype(BF16)
    wkv3 = w_kv_up.reshape(MLA_KV_RANK, nh, nope + MLA_V_DIM)
    wk = jnp.pad(wkv3[..., :nope], ((0, 0), (0, 0), (0, LANES - nope))
                 ).reshape(MLA_KV_RANK, nh * LANES).astype(BF16)
    wvt = wkv3[..., nope:].reshape(MLA_KV_RANK, nh * MLA_V_DIM).T.astype(BF16)
    padg = lambda v: jnp.pad(v, (0, pad_r)).reshape(1, LANES)
    const = lambda *shape: pl.BlockSpec(shape, lambda i: (0,) * len(shape))
    rows = lambda w: pl.BlockSpec((tm, w), lambda i: (i, 0))
    qkw = nh * LANES
    return pl.pallas_call(
        functools.partial(_mla_proj_kernel, tm=tm),
        grid=(t // tm,),
        in_specs=[rows(d), rows(LANES), rows(LANES), const(1, d), const(d, w1.shape[1]),
                  const(1, MLA_Q_RANK), const(1, MLA_KV_RANK), const(MLA_Q_RANK, qkw),
                  const(MLA_Q_RANK, qkw), const(MLA_KV_RANK, qkw), const(MIX_WIDTH, MLA_KV_RANK),
                  const(1, LANES), const(1, LANES), const(1, LANES)],
        out_specs=[rows(qkw), rows(qkw),
                   pl.BlockSpec((1, tm // KV_BLOCK, MIX_WIDTH, KV_BLOCK),
                                lambda i: (i // per_seq, i % per_seq, 0, 0)),
                   rows(MEM_WIDTH)],
        out_shape=[jax.ShapeDtypeStruct((t, qkw), BF16), jax.ShapeDtypeStruct((t, qkw), BF16),
                   jax.ShapeDtypeStruct((t // seq, seq // KV_BLOCK, MIX_WIDTH, KV_BLOCK), BF16),
                   jax.ShapeDtypeStruct((t, MEM_WIDTH), BF16)],
        compiler_params=_params(("parallel",)),
        name="mla_proj",
    )(x2, cos, sin, g.reshape(1, d), w1, qa.reshape(1, -1), kva.reshape(1, -1), wq, wqr, wk, wvt,
      padg(q_gain), padg(k_gain), jnp.tile(m_gain, 2).reshape(1, LANES))


def _ffn_kernel(x_ref, xh_ref, om_ref, omh_ref, oc_ref, och_ref, wo1_ref, wo2_ref, g_ref, wup_ref,
                cw_ref, cb_ref, wdn_ref, o_ref, u_sc, act_sc, *, tm, per_seq):
    i = pl.program_id(0)
    cat = lambda halo_ref, main_ref: jnp.concatenate([halo_ref[...], main_ref[...]], axis=0)
    x_all = (cat(xh_ref, x_ref) + _dot(cat(omh_ref, om_ref), wo1_ref[...])
             + _dot(cat(och_ref, oc_ref), wo2_ref[...]))
    x = x_all[HALO:]
    keep = jnp.where(i % per_seq == 0, 0.0, 1.0)
    row = lax.broadcasted_iota(jnp.int32, (HALO + tm, 1), 0)
    h_all = (_rms(x_all) * g_ref[...] * jnp.where(row < HALO, keep, 1.0)).astype(BF16)
    tf = FF_CHUNK
    n_chunks = D_FF // tf

    def columns(c, part):
        return slice(part * D_FF + c * tf, part * D_FF + (c + 1) * tf)

    def up(c):
        for part in range(2):
            u_sc[c % 2, part] = _dot(h_all, wup_ref[:, columns(c, part)])

    def activate(c):
        halves = []
        for part in range(2):
            cw = cw_ref[:, columns(c, part)]
            u = u_sc.at[c % 2, part]
            halves.append(cw[2:3] * u[HALO:HALO + tm, :] + cw[1:2] * u[HALO - 1:HALO - 1 + tm, :]
                          + cw[0:1] * u[HALO - 2:HALO - 2 + tm, :] + cb_ref[:, columns(c, part)])
        gate, val = halves
        act_sc[:, c * tf:(c + 1) * tf] = (gate * jax.nn.sigmoid(gate) * val).astype(BF16)

    up(0)
    for c in range(n_chunks):
        if c + 1 < n_chunks:
            up(c + 1)
        activate(c)
    o_ref[...] = x + _dot(act_sc[...], wdn_ref[...])


def _out_proj_ffn(x2, o_mix, o_mem, seq, w_out, g, w_up, conv_w, conv_b, w_down):
    t, d = x2.shape
    tm = ROW_TILE
    per_seq = seq // tm
    rows = lambda w: pl.BlockSpec((tm, w), lambda i: (i, 0))
    halo = lambda w: pl.BlockSpec((HALO, w), lambda i: (jnp.maximum(i * (tm // HALO) - 1, 0), 0))
    resident = lambda *shape: pl.BlockSpec(shape, lambda i: (0,) * len(shape),
                                           pipeline_mode=pl.Buffered(1))
    return pl.pallas_call(
        functools.partial(_ffn_kernel, tm=tm, per_seq=per_seq),
        grid=(t // tm,),
        in_specs=[rows(d), halo(d), rows(MIX_WIDTH), halo(MIX_WIDTH), rows(MEM_WIDTH), halo(MEM_WIDTH),
                  resident(MIX_WIDTH, d), resident(MEM_WIDTH, d), resident(1, d),
                  resident(d, 2 * D_FF), resident(3, 2 * D_FF), resident(1, 2 * D_FF),
                  resident(D_FF, d)],
        out_specs=rows(d), out_shape=jax.ShapeDtypeStruct((t, d), F32),
        scratch_shapes=[pltpu.VMEM((2, 2, HALO + tm, FF_CHUNK), F32), pltpu.VMEM((tm, D_FF), BF16)],
        compiler_params=_params(("parallel",)),
        name="out_proj_conv_ffn",
    )(x2, x2, o_mix, o_mix, o_mem, o_mem, w_out[:MIX_WIDTH].astype(BF16),
      w_out[MIX_WIDTH:].astype(BF16), g.reshape(1, d), w_up.astype(BF16), conv_w,
      conv_b.reshape(1, -1), w_down.astype(BF16))


def kernel(x, mem, positions, norm_mix, norm_ffn, w_out, mem_norm, w_mem_kv, mem_q_gain,
           mem_k_gain, fox_w_in, fox_b_f, fox_q_gain, fox_k_gain, mla_w_in, mla_qa_norm,
           mla_kva_norm, mla_w_q_up, mla_w_kv_up, mla_q_gain, mla_k_gain, moba_w_in,
           moba_q_gain, moba_k_gain, ffn_w_up, ffn_conv_w, ffn_conv_b, ffn_w_down):
    b, s, d = x.shape
    depth = norm_mix.shape[0]
    assert d == D_MODEL and s % ROW_TILE == 0 and mem.shape[1] == KV_BLOCK
    n_pairs = N_MIX_HEADS // 2
    k_mem, vt_mem = _mem_prep(mem, mem_norm, w_mem_kv, mem_k_gain)
    x2 = x.reshape(b * s, d)
    rope_cos, rope_sin = _rope_tables(positions) if depth > 1 else (None, None)
    n_blocks = s // MOBA_BLOCK
    slopes = 2.0 ** (-8.0 * jnp.arange(1, N_MIX_HEADS + 1, dtype=F32) / N_MIX_HEADS)
    slopes = jnp.broadcast_to(slopes.reshape(n_pairs, 2, 1), (n_pairs, 2, LANES))

    for i in range(depth):
        kind, j = i % N_MIXERS, i // N_MIXERS
        if kind == 0:
            q, k, vt, qm, flog = _qkv_proj(x2, s, norm_mix[i], fox_w_in[j], fox_q_gain[j],
                                           fox_k_gain[j], mem_q_gain[i], fox_b_f[j])
            cum = _cumsum(flog.reshape(b, s, LANES))[..., :N_MIX_HEADS]
            cum = cum.transpose(0, 2, 1).reshape(b, n_pairs, 2, s)
            o_mix = _attention("fox", q.reshape(b, s, -1), k.reshape(b, s, -1), vt, (cum, cum))
        elif kind == 1:
            q, k, vt, qm = _mla_proj(x2, rope_cos, rope_sin, s, norm_mix[i], mla_w_in[j], mla_qa_norm[j],
                                     mla_kva_norm[j], mla_w_q_up[j], mla_w_kv_up[j],
                                     mla_q_gain[j], mla_k_gain[j], mem_q_gain[i])
            o_mix = _attention("mla", q.reshape(b, s, -1), k.reshape(b, s, -1), vt)
        else:
            q, k, vt, qm = _qkv_proj(x2, s, norm_mix[i], moba_w_in[j], moba_q_gain[j],
                                     moba_k_gain[j], mem_q_gain[i])
            extra = (positions.reshape(b, 1, s), positions.reshape(b, 1, s), slopes)
            o_mix = _attention("moba", q.reshape(b, s, -1), k.reshape(b, s, -1), vt, extra,
                               n_sel=min(MOBA_TOPK, n_blocks - 1))
        o_mem = _attention("mem", qm.reshape(b, s, MEM_WIDTH), k_mem[i], vt_mem)
        x2 = _out_proj_ffn(x2, o_mix.reshape(b * s, MIX_WIDTH), o_mem.reshape(b * s, MEM_WIDTH), s,
                           w_out[i], norm_ffn[i], ffn_w_up[i], ffn_conv_w[i], ffn_conv_b[i],
                           ffn_w_down[i])
    return x2.reshape(b, s, d)
```

```python
import functools

import jax
import jax.numpy as jnp
from jax import lax
from jax.experimental import pallas as pl
from jax.experimental.pallas import tpu as pltpu

F32 = jnp.float32
BF16 = jnp.bfloat16

D_MODEL = 1024
HEAD_DIM = 64
N_MIX_HEADS = 12
MIX_WIDTH = N_MIX_HEADS * HEAD_DIM
N_MEM_HEADS = 4
MEM_WIDTH = N_MEM_HEADS * HEAD_DIM
N_MIXERS = 3
MLA_Q_RANK = 256
MLA_KV_RANK = 128
MLA_NOPE_DIM = 64
MLA_ROPE_DIM = 32
MLA_QK_DIM = MLA_NOPE_DIM + MLA_ROPE_DIM
MLA_V_DIM = 64
ROPE_THETA = 10000.0
MOBA_BLOCK = 256
MOBA_TOPK = 3
D_FF = 2816
EPS = 1e-6

LANES = 128
KV_BLOCK = 256
Q_TILE = 256
ATTN_SUB = 8
ATTN_LOOKAHEAD = 5
ROW_TILE = 512
FF_CHUNK = 256
HALO = 16
MASKED = -1e30
LOG2E = 1.4426950408889634
SEL_LANE = 8
DENOM_ROWS = 16
VMEM_LIMIT = 56 * 1024 * 1024


def _params(semantics):
    return pltpu.CompilerParams(dimension_semantics=semantics, vmem_limit_bytes=VMEM_LIMIT)


def _nt_dot(a, b):
    return lax.dot_general(a, b, (((1,), (1,)), ((), ())), preferred_element_type=F32)


def _dot(a, b):
    return jnp.dot(a, b, preferred_element_type=F32)


def _rms(x):
    return x * lax.rsqrt(jnp.mean(x * x, axis=-1, keepdims=True) + EPS)


def _pair_norm(y, gain):
    lane = lax.broadcasted_iota(jnp.int32, y.shape, 1)
    lo = lane < HEAD_DIM
    y2 = y * y
    s_lo = jnp.sum(jnp.where(lo, y2, 0.0), axis=-1, keepdims=True)
    s_hi = jnp.sum(jnp.where(lo, 0.0, y2), axis=-1, keepdims=True)
    ms = jnp.where(lo, s_lo, s_hi) * (1.0 / HEAD_DIM)
    return y * lax.rsqrt(ms + EPS) * gain


def _slab_norm(y, gain, width):
    ms = jnp.sum(y * y, axis=-1, keepdims=True) * (1.0 / width)
    return y * lax.rsqrt(ms + EPS) * gain


def _split3(x):
    x1 = x.astype(BF16).astype(F32)
    r = x - x1
    x2 = r.astype(BF16).astype(F32)
    return x1, x2, (r - x2).astype(BF16).astype(F32)


def _store_vt(vt_ref, vt, tm):
    for j in range(tm // KV_BLOCK):
        vt_ref[0, j] = vt[:, j * KV_BLOCK:(j + 1) * KV_BLOCK]


def _mem_kernel(mem_ref, g_ref, wk_ref, wvt_ref, kg_ref, kn_ref, vt_ref, *, depth):
    h = (_rms(mem_ref[0]) * g_ref[...]).astype(BF16)
    k = _dot(h, wk_ref[...])
    for s in range(MEM_WIDTH // LANES):
        slab = k[:, s * LANES:(s + 1) * LANES]
        for i in range(depth):
            kn_ref[i, 0, :, s * LANES:(s + 1) * LANES] = _pair_norm(slab, kg_ref[i]).astype(BF16)
    vt_ref[0, 0] = _nt_dot(wvt_ref[...], h).astype(BF16)


def _mem_prep(mem, mem_norm, w_mem_kv, mem_k_gain):
    b, m, d = mem.shape
    depth = mem_k_gain.shape[0]
    wk = w_mem_kv[:, :MEM_WIDTH].astype(BF16)
    wvt = w_mem_kv[:, MEM_WIDTH:].T.astype(BF16)
    kg = jnp.tile(mem_k_gain, (1, 2)).reshape(depth, 1, LANES)
    const = lambda *shape: pl.BlockSpec(shape, lambda i: (0,) * len(shape))
    return pl.pallas_call(
        functools.partial(_mem_kernel, depth=depth),
        grid=(b,),
        in_specs=[pl.BlockSpec((1, m, d), lambda i: (i, 0, 0)), const(1, d), const(d, MEM_WIDTH),
                  const(MEM_WIDTH, d), const(depth, 1, LANES)],
        out_specs=[pl.BlockSpec((depth, 1, m, MEM_WIDTH), lambda i: (0, i, 0, 0)),
                   pl.BlockSpec((1, 1, MEM_WIDTH, m), lambda i: (i, 0, 0, 0))],
        out_shape=[jax.ShapeDtypeStruct((depth, b, m, MEM_WIDTH), BF16),
                   jax.ShapeDtypeStruct((b, 1, MEM_WIDTH, m), BF16)],
        compiler_params=_params(("parallel",)),
        name="mem_prep",
    )(mem, mem_norm.reshape(1, d), wk, wvt, kg)


def _qkv_proj_kernel(x_ref, g_ref, wqk_ref, wvt_ref, wm_ref, qg_ref, kg_ref, mg_ref, *rest,
                     has_gate, tm):
    if has_gate:
        wf_ref, bf_ref, q_ref, k_ref, vt_ref, qm_ref, f_ref = rest
    else:
        q_ref, k_ref, vt_ref, qm_ref = rest
    h = (_rms(x_ref[...]) * g_ref[...]).astype(BF16)
    for c in range(0, 2 * MIX_WIDTH, 2 * LANES):
        y = _dot(h, wqk_ref[:, c:c + 2 * LANES])
        for s in range(2):
            col = c + s * LANES
            slab = y[:, s * LANES:(s + 1) * LANES]
            if col < MIX_WIDTH:
                q_ref[:, col:col + LANES] = (
                    _pair_norm(slab, qg_ref[...]) * (HEAD_DIM ** -0.5 * LOG2E)).astype(BF16)
            else:
                k_ref[:, col - MIX_WIDTH:col - MIX_WIDTH + LANES] = (
                    _pair_norm(slab, kg_ref[...]).astype(BF16))
    _store_vt(vt_ref, _nt_dot(wvt_ref[...], h).astype(BF16), tm)
    ym = _dot(h, wm_ref[...])
    for s in range(MEM_WIDTH // LANES):
        qm_ref[:, s * LANES:(s + 1) * LANES] = (
            _pair_norm(ym[:, s * LANES:(s + 1) * LANES], mg_ref[...]) * (HEAD_DIM ** -0.5 * LOG2E)
        ).astype(BF16)
    if has_gate:
        f = _dot(h, wf_ref[...]) + bf_ref[...]
        f_ref[...] = jnp.minimum(f, 0.0) - jnp.log1p(jnp.exp(-jnp.abs(f)))


def _qkv_proj(x2, seq, g, w_in, q_gain, k_gain, m_gain, b_f=None):
    t, d = x2.shape
    tm = ROW_TILE
    nblk = seq // KV_BLOCK
    per_seq = seq // tm
    has_gate = b_f is not None
    wqk = w_in[:, :2 * MIX_WIDTH].astype(BF16)
    wvt = w_in[:, 2 * MIX_WIDTH:3 * MIX_WIDTH].T.astype(BF16)
    wm = w_in[:, -MEM_WIDTH:].astype(BF16)
    pair = lambda v: jnp.tile(v, 2).reshape(1, LANES)
    const = lambda *shape: pl.BlockSpec(shape, lambda i: (0,) * len(shape))
    rows = lambda w: pl.BlockSpec((tm, w), lambda i: (i, 0))
    args = [x2, g.reshape(1, d), wqk, wvt, wm, pair(q_gain), pair(k_gain), pair(m_gain)]
    in_specs = [rows(d), const(1, d), const(d, 2 * MIX_WIDTH), const(MIX_WIDTH, d),
                const(d, MEM_WIDTH), const(1, LANES), const(1, LANES), const(1, LANES)]
    out_specs = [rows(MIX_WIDTH), rows(MIX_WIDTH),
                 pl.BlockSpec((1, tm // KV_BLOCK, MIX_WIDTH, KV_BLOCK),
                              lambda i: (i // per_seq, i % per_seq, 0, 0)),
                 rows(MEM_WIDTH)]
    out_shape = [jax.ShapeDtypeStruct((t, MIX_WIDTH), BF16), jax.ShapeDtypeStruct((t, MIX_WIDTH), BF16),
                 jax.ShapeDtypeStruct((t // seq, nblk, MIX_WIDTH, KV_BLOCK), BF16),
                 jax.ShapeDtypeStruct((t, MEM_WIDTH), BF16)]
    if has_gate:
        nh = N_MIX_HEADS
        wf = jnp.pad(w_in[:, 3 * MIX_WIDTH:3 * MIX_WIDTH + nh], ((0, 0), (0, LANES - nh))).astype(BF16)
        args += [wf, jnp.pad(b_f, (0, LANES - nh)).reshape(1, LANES)]
        in_specs += [const(d, LANES), const(1, LANES)]
        out_specs.append(rows(LANES))
        out_shape.append(jax.ShapeDtypeStruct((t, LANES), F32))
    return pl.pallas_call(
        functools.partial(_qkv_proj_kernel, has_gate=has_gate, tm=tm),
        grid=(t // tm,), in_specs=in_specs, out_specs=out_specs, out_shape=out_shape,
        compiler_params=_params(("parallel",)),
        name="qkv_proj_gate" if has_gate else "qkv_proj",
    )(*args)


def _cumsum_kernel(f_ref, o_ref, *, seq):
    r = lax.broadcasted_iota(jnp.int32, (KV_BLOCK, KV_BLOCK), 0)
    c = lax.broadcasted_iota(jnp.int32, (KV_BLOCK, KV_BLOCK), 1)
    tri = jnp.where(c <= r, 1.0, 0.0).astype(BF16)
    carry = jnp.zeros((1, LANES), F32)
    for j in range(seq // KV_BLOCK):
        x = f_ref[0, j * KV_BLOCK:(j + 1) * KV_BLOCK, :]
        x1 = x.astype(BF16)
        r1 = x - x1.astype(F32)
        x2 = r1.astype(BF16)
        x3 = (r1 - x2.astype(F32)).astype(BF16)
        cs = _dot(tri, x1) + _dot(tri, x2) + _dot(tri, x3) + carry
        o_ref[0, j * KV_BLOCK:(j + 1) * KV_BLOCK, :] = cs
        carry = cs[KV_BLOCK - 1:KV_BLOCK, :]


def _cumsum(flog):
    b, seq, w = flog.shape
    spec = pl.BlockSpec((1, seq, w), lambda i: (i, 0, 0))
    return pl.pallas_call(
        functools.partial(_cumsum_kernel, seq=seq),
        grid=(b,), in_specs=[spec], out_specs=spec,
        out_shape=jax.ShapeDtypeStruct((b, seq, w), F32),
        compiler_params=_params(("parallel",)),
        name="gate_cumsum",
    )(flog)


def _attn_kernel(*refs, mode, tq, sub, seq, n_sel):
    if mode == "fox":
        q_ref, k_ref, vt_ref, cq_ref, ck_ref, o_ref, m_sc, acc_sc, kaug_sc = refs
    elif mode == "moba":
        (q_ref, k_ref, vt_ref, pq_ref, pk_ref, sl_ref, o_ref,
         m_sc, acc_sc, kaug_sc, km_sc) = refs
    else:
        q_ref, k_ref, vt_ref, o_ref, m_sc, acc_sc = refs
    qi = pl.program_id(2)
    tk = KV_BLOCK
    nblk = seq // tk

    def spare(h):
        return HEAD_DIM * (1 - h)

    def own_lanes(h, shape):
        lane = lax.broadcasted_iota(jnp.int32, shape, 1)
        return (lane >= HEAD_DIM * h) & (lane < HEAD_DIM * (h + 1))

    def spare_slab(h, term, term_first, extra):
        n = term.shape[1]
        t1, t2, t3 = _split3(term)
        sl8 = lax.broadcasted_iota(jnp.int32, (8, n), 0)
        t_at, one_at = (0, 3) if term_first else (3, 0)
        head = jnp.where(sl8 == t_at, t1, jnp.where(sl8 == t_at + 1, t2, jnp.where(
            sl8 == t_at + 2, t3, jnp.where((sl8 >= one_at) & (sl8 < one_at + 3), 1.0, 0.0))))
        pieces = [head] if extra is None else [head, extra]
        used = sum(p.shape[0] for p in pieces)
        if spare(h):
            pieces.insert(0, jnp.zeros((spare(h), n), F32))
        pieces.append(jnp.zeros((LANES - spare(h) - used, n), F32))
        return jnp.concatenate(pieces, axis=0).T

    def key_side(h, col, first_key):
        n = col.shape[1]
        extra = None
        if mode == "moba":
            nbp = km_sc.shape[1]
            key = lax.broadcasted_iota(jnp.int32, (nbp, n), 1) + first_key
            blk = lax.shift_right_logical(key, tk.bit_length() - 1)
            extra = jnp.where(blk == lax.broadcasted_iota(jnp.int32, (nbp, n), 0), 1.0, 0.0)
        aug = spare_slab(h, col, True, extra)
        k = k_ref[0, first_key:first_key + n, :].astype(F32)
        return jnp.where(own_lanes(h, k.shape), k, aug).astype(BF16)

    chunk = 4 * tk
    if mode == "fox":
        @pl.when(qi == 0)
        def _():
            for h in range(2):
                for c in range(0, seq, chunk):
                    ck = ck_ref[0, 0, h:h + 1, c:c + chunk] * -LOG2E
                    kaug_sc[h, c:c + chunk, :] = key_side(h, ck, c)
    if mode == "moba":
        pos0 = pk_ref[0, :, 0:1]
        @pl.when(qi == 0)
        def _():
            for c in range(0, seq, chunk):
                pk = (pk_ref[0, :, c:c + chunk] - pos0).astype(F32)
                for h in range(2):
                    col = pk * (sl_ref[0, h:h + 1, 0:1] * LOG2E)
                    kaug_sc[h, c:c + chunk, :] = key_side(h, col, c)
            nbp = km_sc.shape[1]
            blk = lax.broadcasted_iota(jnp.int32, (nbp, seq), 0)
            key = lax.broadcasted_iota(jnp.int32, (nbp, seq), 1)
            pool = jnp.where((key >= blk * tk) & (key < (blk + 1) * tk), 1.0, 0.0).astype(BF16)
            km = _dot(pool, k_ref[0]) * (1.0 / tk)
            hi = km.astype(BF16)
            km_sc[0] = hi
            km_sc[1] = (km - hi.astype(F32)).astype(BF16)

    blk0 = qi * sub
    chains = [(i, h) for i in range(sub) for h in range(2)]

    def selection_bias(i, q_masked):
        nbp = km_sc.shape[1]
        n_iota = lax.broadcasted_iota(jnp.int32, (nbp, tq), 0)
        g = _nt_dot(km_sc[0], q_masked) + _nt_dot(km_sc[1], q_masked)
        g = jnp.where(n_iota < blk0 + i, g, -jnp.inf)
        bias = jnp.where(n_iota == blk0 + i, 0.0, MASKED)
        for _ in range(n_sel):
            mx = jnp.max(g, axis=0, keepdims=True)
            idx = jnp.min(jnp.where(g == mx, n_iota, nbp), axis=0, keepdims=True)
            hit = n_iota == idx
            ok = (mx > -jnp.inf) & (mx < jnp.inf)
            bias = jnp.where(hit & ok, 0.0, bias)
            g = jnp.where(hit, -jnp.inf, g)
        return bias

    def query_side(h, q_masked, row_term, sel):
        aug = spare_slab(h, row_term, False, sel)
        return jnp.where(own_lanes(h, aug.shape), q_masked, aug).astype(BF16)

    qh = {}
    for i in range(sub):
        qb = q_ref[0, i * tq:(i + 1) * tq, :]
        if mode == "mla":
            qh[i, 0], qh[i, 1] = qb[:, :LANES], qb[:, LANES:]
            continue
        qf = qb.astype(F32)
        for h in range(2):
            q_masked = jnp.where(own_lanes(h, qf.shape), qf, 0.0)
            if mode == "mem":
                qh[i, h] = q_masked.astype(BF16)
            elif mode == "fox":
                row_term = cq_ref[0, 0, h:h + 1, i * tq:(i + 1) * tq] * LOG2E
                qh[i, h] = query_side(h, q_masked, row_term, None)
            else:
                pq = (pq_ref[0, :, i * tq:(i + 1) * tq] - pos0).astype(F32)
                row_term = -((sl_ref[0, h:h + 1, 0:1] * LOG2E) * pq)
                sel = selection_bias(i, q_masked.astype(BF16)) if n_sel > 0 else None
                qh[i, h] = query_side(h, q_masked, row_term, sel)

    def logits(step):
        i, h, j, own = step
        start = j * tk if isinstance(j, int) else pl.multiple_of(j * tk, tk)
        if mode == "mla":
            kb = k_ref[0, pl.ds(start, tk), h * LANES:(h + 1) * LANES]
        elif mode == "mem":
            kb = k_ref[0, pl.ds(start, tk), :]
        else:
            kb = kaug_sc[h, pl.ds(start, tk), :]
        z = _nt_dot(kb, qh[i, h])
        if own and mode != "mem":
            key = lax.broadcasted_iota(jnp.int32, (tk, tq), 0)
            qry = lax.broadcasted_iota(jnp.int32, (tk, tq), 1)
            z = jnp.where(key <= qry, z, MASKED)
        return z

    ones_rows = jnp.ones((DENOM_ROWS, tk), BF16)

    def fold(step, z):
        i, h, j, own = step
        vt = jnp.concatenate([vt_ref[0, j, h * HEAD_DIM:(h + 1) * HEAD_DIM, :], ones_rows], axis=0)
        mz = jnp.max(z, axis=0, keepdims=True)
        if own:
            m_sc[i, h] = mz
            acc_sc[i, h] = _dot(vt, jnp.exp2(z - mz).astype(BF16))
        else:
            m_prev = m_sc[i, h]
            m_new = jnp.maximum(m_prev, mz)
            alpha = jnp.exp2(m_prev - m_new)
            acc_sc[i, h] = alpha * acc_sc[i, h] + _dot(vt, jnp.exp2(z - m_new).astype(BF16))
            m_sc[i, h] = m_new

    def run(steps):
        zs = {s: logits(steps[s]) for s in range(min(ATTN_LOOKAHEAD, len(steps)))}
        for s in range(len(steps)):
            if s + ATTN_LOOKAHEAD < len(steps):
                zs[s + ATTN_LOOKAHEAD] = logits(steps[s + ATTN_LOOKAHEAD])
            fold(steps[s], zs.pop(s))

    if mode == "mem":
        run([(i, h, 0, True) for i, h in chains])
    else:
        steps = [(i, h, blk0 + i, True) for i, h in chains]
        steps += [(i, h, blk0 + b, False) for b in range(sub - 1) for i, h in chains if i > b]
        run(steps)

        def body(g, carry):
            run([(i, h, g * sub + b, False) for b in range(sub) for i, h in chains])
            return carry
        lax.fori_loop(0, qi, body, 0)

    for i in range(sub):
        out_t = jnp.concatenate(
            [acc_sc[i, h, :HEAD_DIM, :] / acc_sc[i, h, HEAD_DIM:HEAD_DIM + 1, :] for h in range(2)],
            axis=0)
        o_ref[0, i * tq:(i + 1) * tq, :] = out_t.T.astype(BF16)


def _attention(mode, q, k, vt, extra=(), n_sel=0):
    b, s, _ = q.shape
    sk = k.shape[1]
    n_pairs = vt.shape[2] // LANES
    sub = ATTN_SUB
    tq = Q_TILE
    ts = sub * tq
    assert s % ts == 0
    qw = 2 * LANES if mode == "mla" else LANES
    in_specs = [pl.BlockSpec((1, ts, qw), lambda bi, p, i: (bi, i, p)),
                pl.BlockSpec((1, sk, qw), lambda bi, p, i: (bi, 0, p)),
                pl.BlockSpec((1, sk // KV_BLOCK, LANES, KV_BLOCK), lambda bi, p, i: (bi, 0, p, 0))]
    scratch = [pltpu.VMEM((sub, 2, 1, tq), F32),
               pltpu.VMEM((sub, 2, HEAD_DIM + DENOM_ROWS, tq), F32)]
    if mode == "fox":
        in_specs += [pl.BlockSpec((1, 1, 2, ts), lambda bi, p, i: (bi, p, 0, i)),
                     pl.BlockSpec((1, 1, 2, s), lambda bi, p, i: (bi, p, 0, 0))]
        scratch += [pltpu.VMEM((2, s, LANES), BF16)]
    elif mode == "moba":
        nbp = -(-(s // KV_BLOCK) // 16) * 16
        in_specs += [pl.BlockSpec((1, 1, ts), lambda bi, p, i: (bi, 0, i)),
                     pl.BlockSpec((1, 1, s), lambda bi, p, i: (bi, 0, 0)),
                     pl.BlockSpec((1, 2, LANES), lambda bi, p, i: (p, 0, 0))]
        assert SEL_LANE + nbp <= HEAD_DIM
        scratch += [pltpu.VMEM((2, s, LANES), BF16), pltpu.VMEM((2, nbp, LANES), BF16)]
    return pl.pallas_call(
        functools.partial(_attn_kernel, mode=mode, tq=tq, sub=sub, seq=sk, n_sel=n_sel),
        grid=(b, n_pairs, s // ts),
        in_specs=in_specs,
        out_specs=pl.BlockSpec((1, ts, LANES), lambda bi, p, i: (bi, i, p)),
        out_shape=jax.ShapeDtypeStruct((b, s, n_pairs * LANES), BF16),
        scratch_shapes=scratch,
        compiler_params=_params(("parallel", "parallel", "arbitrary")),
        name="attn_" + mode,
    )(q, k, vt, *extra)


def _rope_table_kernel(pos_ref, invf_ref, cos_ref, sin_ref):
    ang = pos_ref[...].astype(F32) * invf_ref[...]
    cos_ref[...] = jnp.cos(ang)
    sin_ref[...] = jnp.sin(ang)


def _rope_tables(positions):
    t = positions.size
    half = MLA_ROPE_DIM // 2
    per_row = LANES // half
    inv_freq = ROPE_THETA ** (-jnp.arange(0, MLA_ROPE_DIM, 2, dtype=F32) / MLA_ROPE_DIM)
    pos = jnp.repeat(positions.reshape(t // per_row, per_row), half, axis=1)
    rows = min(t // per_row, 1024)
    spec = pl.BlockSpec((rows, LANES), lambda i: (i, 0))
    cos, sin = pl.pallas_call(
        _rope_table_kernel, grid=(t // per_row // rows,),
        in_specs=[spec, pl.BlockSpec((1, LANES), lambda i: (0, 0))],
        out_specs=[spec, spec],
        out_shape=[jax.ShapeDtypeStruct((t // per_row, LANES), F32)] * 2,
        compiler_params=_params(("parallel",)),
        name="rope_tables",
    )(pos, jnp.tile(inv_freq, per_row).reshape(1, LANES))
    pad_r = LANES - MLA_NOPE_DIM - MLA_ROPE_DIM
    slab = lambda v, fill: jnp.concatenate(
        [jnp.full((t, MLA_NOPE_DIM), fill, F32), v.reshape(t, half), v.reshape(t, half),
         jnp.full((t, pad_r), fill, F32)], axis=1)
    return slab(cos, 1.0), slab(sin, 0.0)


def _mla_proj_kernel(x_ref, cos_ref, sin_ref, g_ref, w1_ref, qa_ref, kva_ref, wq_ref, wqr_ref, wk_ref,
                     wvt_ref, qg_ref, kg_ref, mg_ref, q_ref, k_ref, vt_ref, qm_ref, *, tm):
    h = (_rms(x_ref[...]) * g_ref[...]).astype(BF16)
    p = _dot(h, w1_ref[...])
    o = MLA_Q_RANK
    q_lat = p[:, :o]
    kv_lat = p[:, o:o + LANES]
    kr = p[:, o + LANES:o + 2 * LANES]
    kr_rot = p[:, o + 2 * LANES:o + 3 * LANES]
    ym = p[:, o + 3 * LANES:]
    qn = (_rms(q_lat) * qa_ref[...]).astype(BF16)
    kvn = (_rms(kv_lat) * kva_ref[...]).astype(BF16)
    cos, sin = cos_ref[...], sin_ref[...]
    k_rope = kr * cos + kr_rot * sin
    width = N_MIX_HEADS * LANES
    for c in range(0, width, 2 * LANES):
        qc = _dot(qn, wq_ref[:, c:c + 2 * LANES])
        qr = _dot(qn, wqr_ref[:, c:c + 2 * LANES])
        kc = _dot(kvn, wk_ref[:, c:c + 2 * LANES])
        for s in range(2):
            sl = slice(s * LANES, (s + 1) * LANES)
            out = slice(c + s * LANES, c + (s + 1) * LANES)
            qq = qc[:, sl] * cos + qr[:, sl] * sin
            q_ref[:, out] = (_slab_norm(qq, qg_ref[...], MLA_QK_DIM)
                             * (MLA_QK_DIM ** -0.5 * LOG2E)).astype(BF16)
            k_ref[:, out] = _slab_norm(kc[:, sl] + k_rope, kg_ref[...], MLA_QK_DIM).astype(BF16)
    _store_vt(vt_ref, _nt_dot(wvt_ref[...], kvn).astype(BF16), tm)
    for s in range(MEM_WIDTH // LANES):
        qm_ref[:, s * LANES:(s + 1) * LANES] = (
            _pair_norm(ym[:, s * LANES:(s + 1) * LANES], mg_ref[...]) * (HEAD_DIM ** -0.5 * LOG2E)
        ).astype(BF16)


def _rot_half(w):
    half = MLA_ROPE_DIM // 2
    return jnp.concatenate([-w[..., half:], w[..., :half]], axis=-1)


def _mla_proj(x2, cos, sin, seq, g, w_in, qa, kva, w_q_up, w_kv_up, q_gain, k_gain, m_gain):
    t, d = x2.shape
    tm = ROW_TILE
    per_seq = seq // tm
    nh, nope, rope_d = N_MIX_HEADS, MLA_NOPE_DIM, MLA_ROPE_DIM
    pad_r = LANES - nope - rope_d
    o = MLA_Q_RANK + MLA_KV_RANK
    k_r = w_in[:, o:o + rope_d]
    slab = lambda w: jnp.pad(w, ((0, 0), (nope, pad_r)))
    w1 = jnp.concatenate([w_in[:, :o], slab(k_r), slab(_rot_half(k_r)), w_in[:, -MEM_WIDTH:]],
                         axis=1).astype(BF16)
    wq3 = w_q_up.reshape(MLA_Q_RANK, nh, MLA_QK_DIM)
    wq = jnp.pad(wq3, ((0, 0), (0, 0), (0, pad_r))).reshape(MLA_Q_RANK, nh * LANES).astype(BF16)
    wqr = jnp.pad(_rot_half(wq3[..., nope:]), ((0, 0), (0, 0), (nope, pad_r))
                  ).reshape(MLA_Q_RANK, nh * LANES).astype(BF16)
    wkv3 = w_kv_up.reshape(MLA_KV_RANK, nh, nope + MLA_V_DIM)
    wk = jnp.pad(wkv3[..., :nope], ((0, 0), (0, 0), (0, LANES - nope))
                 ).reshape(MLA_KV_RANK, nh * LANES).astype(BF16)
    wvt = wkv3[..., nope:].reshape(MLA_KV_RANK, nh * MLA_V_DIM).T.astype(BF16)
    padg = lambda v: jnp.pad(v, (0, pad_r)).reshape(1, LANES)
    const = lambda *shape: pl.BlockSpec(shape, lambda i: (0,) * len(shape))
    rows = lambda w: pl.BlockSpec((tm, w), lambda i: (i, 0))
    qkw = nh * LANES
    return pl.pallas_call(
        functools.partial(_mla_proj_kernel, tm=tm),
        grid=(t // tm,),
        in_specs=[rows(d), rows(LANES), rows(LANES), const(1, d), const(d, w1.shape[1]),
                  const(1, MLA_Q_RANK), const(1, MLA_KV_RANK), const(MLA_Q_RANK, qkw),
                  const(MLA_Q_RANK, qkw), const(MLA_KV_RANK, qkw), const(MIX_WIDTH, MLA_KV_RANK),
                  const(1, LANES), const(1, LANES), const(1, LANES)],
        out_specs=[rows(qkw), rows(qkw),
                   pl.BlockSpec((1, tm // KV_BLOCK, MIX_WIDTH, KV_BLOCK),
                                lambda i: (i // per_seq, i % per_seq, 0, 0)),
                   rows(MEM_WIDTH)],
        out_shape=[jax.ShapeDtypeStruct((t, qkw), BF16), jax.ShapeDtypeStruct((t, qkw), BF16),
                   jax.ShapeDtypeStruct((t // seq, seq // KV_BLOCK, MIX_WIDTH, KV_BLOCK), BF16),
                   jax.ShapeDtypeStruct((t, MEM_WIDTH), BF16)],
        compiler_params=_params(("parallel",)),
        name="mla_proj",
    )(x2, cos, sin, g.reshape(1, d), w1, qa.reshape(1, -1), kva.reshape(1, -1), wq, wqr, wk, wvt,
      padg(q_gain), padg(k_gain), jnp.tile(m_gain, 2).reshape(1, LANES))


def _ffn_kernel(x_ref, xh_ref, om_ref, omh_ref, oc_ref, och_ref, wo1_ref, wo2_ref, g_ref, wup_ref,
                cw_ref, cb_ref, wdn_ref, o_ref, u_sc, act_sc, *, tm, per_seq):
    i = pl.program_id(0)
    cat = lambda halo_ref, main_ref: jnp.concatenate([halo_ref[...], main_ref[...]], axis=0)
    x_all = (cat(xh_ref, x_ref) + _dot(cat(omh_ref, om_ref), wo1_ref[...])
             + _dot(cat(och_ref, oc_ref), wo2_ref[...]))
    x = x_all[HALO:]
    keep = jnp.where(i % per_seq == 0, 0.0, 1.0)
    row = lax.broadcasted_iota(jnp.int32, (HALO + tm, 1), 0)
    h_all = (_rms(x_all) * g_ref[...] * jnp.where(row < HALO, keep, 1.0)).astype(BF16)
    tf = FF_CHUNK
    n_chunks = D_FF // tf

    def columns(c, part):
        return slice(part * D_FF + c * tf, part * D_FF + (c + 1) * tf)

    def up(c):
        for part in range(2):
            u_sc[c % 2, part] = _dot(h_all, wup_ref[:, columns(c, part)])

    def activate(c):
        halves = []
        for part in range(2):
            cw = cw_ref[:, columns(c, part)]
            u = u_sc.at[c % 2, part]
            halves.append(cw[2:3] * u[HALO:HALO + tm, :] + cw[1:2] * u[HALO - 1:HALO - 1 + tm, :]
                          + cw[0:1] * u[HALO - 2:HALO - 2 + tm, :] + cb_ref[:, columns(c, part)])
        gate, val = halves
        act_sc[:, c * tf:(c + 1) * tf] = (gate * jax.nn.sigmoid(gate) * val).astype(BF16)

    up(0)
    for c in range(n_chunks):
        if c + 1 < n_chunks:
            up(c + 1)
        activate(c)
    o_ref[...] = x + _dot(act_sc[...], wdn_ref[...])


def _out_proj_ffn(x2, o_mix, o_mem, seq, layer, w_out, g, w_up, conv_w, conv_b, w_down):
    t, d = x2.shape
    tm = ROW_TILE
    per_seq = seq // tm
    rows = lambda w: pl.BlockSpec((tm, w), lambda i: (i, 0))
    halo = lambda w: pl.BlockSpec((HALO, w), lambda i: (jnp.maximum(i * (tm // HALO) - 1, 0), 0))
    resident = lambda r, c, blk=0: pl.BlockSpec((None, r, c), lambda i: (layer, blk, 0),
                                                pipeline_mode=pl.Buffered(1))
    return pl.pallas_call(
        functools.partial(_ffn_kernel, tm=tm, per_seq=per_seq),
        grid=(t // tm,),
        in_specs=[rows(d), halo(d), rows(MIX_WIDTH), halo(MIX_WIDTH), rows(MEM_WIDTH), halo(MEM_WIDTH),
                  resident(MIX_WIDTH, d), resident(MEM_WIDTH, d, MIX_WIDTH // MEM_WIDTH),
                  resident(1, d), resident(d, 2 * D_FF), resident(3, 2 * D_FF),
                  resident(1, 2 * D_FF), resident(D_FF, d)],
        out_specs=rows(d), out_shape=jax.ShapeDtypeStruct((t, d), F32),
        scratch_shapes=[pltpu.VMEM((2, 2, HALO + tm, FF_CHUNK), F32), pltpu.VMEM((tm, D_FF), BF16)],
        compiler_params=_params(("parallel",)),
        name="out_proj_conv_ffn",
    )(x2, x2, o_mix, o_mix, o_mem, o_mem, w_out, w_out, g, w_up, conv_w, conv_b, w_down)


def kernel(x, mem, positions, norm_mix, norm_ffn, w_out, mem_norm, w_mem_kv, mem_q_gain,
           mem_k_gain, fox_w_in, fox_b_f, fox_q_gain, fox_k_gain, mla_w_in, mla_qa_norm,
           mla_kva_norm, mla_w_q_up, mla_w_kv_up, mla_q_gain, mla_k_gain, moba_w_in,
           moba_q_gain, moba_k_gain, ffn_w_up, ffn_conv_w, ffn_conv_b, ffn_w_down):
    b, s, d = x.shape
    depth = norm_mix.shape[0]
    assert d == D_MODEL and s % ROW_TILE == 0 and mem.shape[1] == KV_BLOCK
    n_pairs = N_MIX_HEADS // 2
    k_mem, vt_mem = _mem_prep(mem, mem_norm, w_mem_kv, mem_k_gain)
    x2 = x.reshape(b * s, d)
    rope_cos, rope_sin = _rope_tables(positions) if depth > 1 else (None, None)
    n_blocks = s // MOBA_BLOCK
    slopes = 2.0 ** (-8.0 * jnp.arange(1, N_MIX_HEADS + 1, dtype=F32) / N_MIX_HEADS)
    slopes = jnp.broadcast_to(slopes.reshape(n_pairs, 2, 1), (n_pairs, 2, LANES))

    ffn_weights = (w_out.astype(BF16), norm_ffn.reshape(depth, 1, d), ffn_w_up.astype(BF16),
                   ffn_conv_w, ffn_conv_b.reshape(depth, 1, -1), ffn_w_down.astype(BF16))

    for i in range(depth):
        kind, j = i % N_MIXERS, i // N_MIXERS
        if kind == 0:
            q, k, vt, qm, flog = _qkv_proj(x2, s, norm_mix[i], fox_w_in[j], fox_q_gain[j],
                                           fox_k_gain[j], mem_q_gain[i], fox_b_f[j])
            cum = _cumsum(flog.reshape(b, s, LANES))[..., :N_MIX_HEADS]
            cum = cum.transpose(0, 2, 1).reshape(b, n_pairs, 2, s)
            o_mix = _attention("fox", q.reshape(b, s, -1), k.reshape(b, s, -1), vt, (cum, cum))
        elif kind == 1:
            q, k, vt, qm = _mla_proj(x2, rope_cos, rope_sin, s, norm_mix[i], mla_w_in[j], mla_qa_norm[j],
                                     mla_kva_norm[j], mla_w_q_up[j], mla_w_kv_up[j],
                                     mla_q_gain[j], mla_k_gain[j], mem_q_gain[i])
            o_mix = _attention("mla", q.reshape(b, s, -1), k.reshape(b, s, -1), vt)
        else:
            q, k, vt, qm = _qkv_proj(x2, s, norm_mix[i], moba_w_in[j], moba_q_gain[j],
                                     moba_k_gain[j], mem_q_gain[i])
            extra = (positions.reshape(b, 1, s), positions.reshape(b, 1, s), slopes)
            o_mix = _attention("moba", q.reshape(b, s, -1), k.reshape(b, s, -1), vt, extra,
                               n_sel=min(MOBA_TOPK, n_blocks - 1))
        o_mem = _attention("mem", qm.reshape(b, s, MEM_WIDTH), k_mem[i], vt_mem)
        x2 = _out_proj_ffn(x2, o_mix.reshape(b * s, MIX_WIDTH), o_mem.reshape(b * s, MEM_WIDTH), s,
                           i, *ffn_weights)
    return x2.reshape(b, s, d)
```

```python
import functools

import jax
import jax.numpy as jnp
from jax import lax
from jax.experimental import pallas as pl
from jax.experimental.pallas import tpu as pltpu

F32 = jnp.float32
BF16 = jnp.bfloat16

D_MODEL = 1024
HEAD_DIM = 64
N_MIX_HEADS = 12
MIX_WIDTH = N_MIX_HEADS * HEAD_DIM
N_MEM_HEADS = 4
MEM_WIDTH = N_MEM_HEADS * HEAD_DIM
N_MIXERS = 3
MLA_Q_RANK = 256
MLA_KV_RANK = 128
MLA_NOPE_DIM = 64
MLA_ROPE_DIM = 32
MLA_QK_DIM = MLA_NOPE_DIM + MLA_ROPE_DIM
MLA_V_DIM = 64
ROPE_THETA = 10000.0
MOBA_BLOCK = 256
MOBA_TOPK = 3
D_FF = 2816
EPS = 1e-6

LANES = 128
KV_BLOCK = 256
Q_TILE = 256
ATTN_SUB = 16
ATTN_LOOKAHEAD = 5
ROW_TILE = 512
FF_CHUNK = 256
HALO = 16
MASKED = -1e30
LOG2E = 1.4426950408889634
SEL_LANE = 8
DENOM_ROWS = 16
VMEM_LIMIT = 56 * 1024 * 1024


def _params(semantics):
    return pltpu.CompilerParams(dimension_semantics=semantics, vmem_limit_bytes=VMEM_LIMIT)


def _nt_dot(a, b):
    return lax.dot_general(a, b, (((1,), (1,)), ((), ())), preferred_element_type=F32)


def _dot(a, b):
    return jnp.dot(a, b, preferred_element_type=F32)


def _rms(x):
    return x * lax.rsqrt(jnp.mean(x * x, axis=-1, keepdims=True) + EPS)


def _pair_norm(y, gain):
    lane = lax.broadcasted_iota(jnp.int32, y.shape, 1)
    lo = lane < HEAD_DIM
    y2 = y * y
    s_lo = jnp.sum(jnp.where(lo, y2, 0.0), axis=-1, keepdims=True)
    s_hi = jnp.sum(jnp.where(lo, 0.0, y2), axis=-1, keepdims=True)
    ms = jnp.where(lo, s_lo, s_hi) * (1.0 / HEAD_DIM)
    return y * lax.rsqrt(ms + EPS) * gain


def _slab_norm(y, gain, width):
    ms = jnp.sum(y * y, axis=-1, keepdims=True) * (1.0 / width)
    return y * lax.rsqrt(ms + EPS) * gain


def _split3(x):
    x1 = x.astype(BF16).astype(F32)
    r = x - x1
    x2 = r.astype(BF16).astype(F32)
    return x1, x2, (r - x2).astype(BF16).astype(F32)


def _store_vt(vt_ref, vt, tm):
    for j in range(tm // KV_BLOCK):
        vt_ref[0, j] = vt[:, j * KV_BLOCK:(j + 1) * KV_BLOCK]


def _mem_kernel(mem_ref, g_ref, wk_ref, wvt_ref, kg_ref, kn_ref, vt_ref, *, depth):
    h = (_rms(mem_ref[0]) * g_ref[...]).astype(BF16)
    k = _dot(h, wk_ref[...])
    for s in range(MEM_WIDTH // LANES):
        slab = k[:, s * LANES:(s + 1) * LANES]
        for i in range(depth):
            kn_ref[i, 0, :, s * LANES:(s + 1) * LANES] = _pair_norm(slab, kg_ref[i]).astype(BF16)
    vt_ref[0, 0] = _nt_dot(wvt_ref[...], h).astype(BF16)


def _mem_prep(mem, mem_norm, w_mem_kv, mem_k_gain):
    b, m, d = mem.shape
    depth = mem_k_gain.shape[0]
    wk = w_mem_kv[:, :MEM_WIDTH].astype(BF16)
    wvt = w_mem_kv[:, MEM_WIDTH:].T.astype(BF16)
    kg = jnp.tile(mem_k_gain, (1, 2)).reshape(depth, 1, LANES)
    const = lambda *shape: pl.BlockSpec(shape, lambda i: (0,) * len(shape))
    return pl.pallas_call(
        functools.partial(_mem_kernel, depth=depth),
        grid=(b,),
        in_specs=[pl.BlockSpec((1, m, d), lambda i: (i, 0, 0)), const(1, d), const(d, MEM_WIDTH),
                  const(MEM_WIDTH, d), const(depth, 1, LANES)],
        out_specs=[pl.BlockSpec((depth, 1, m, MEM_WIDTH), lambda i: (0, i, 0, 0)),
                   pl.BlockSpec((1, 1, MEM_WIDTH, m), lambda i: (i, 0, 0, 0))],
        out_shape=[jax.ShapeDtypeStruct((depth, b, m, MEM_WIDTH), BF16),
                   jax.ShapeDtypeStruct((b, 1, MEM_WIDTH, m), BF16)],
        compiler_params=_params(("parallel",)),
        name="mem_prep",
    )(mem, mem_norm.reshape(1, d), wk, wvt, kg)


def _qkv_proj_kernel(x_ref, g_ref, wqk_ref, wvt_ref, wm_ref, qg_ref, kg_ref, mg_ref, *rest,
                     has_gate, tm):
    if has_gate:
        wf_ref, bf_ref, q_ref, k_ref, vt_ref, qm_ref, f_ref = rest
    else:
        q_ref, k_ref, vt_ref, qm_ref = rest
    h = (_rms(x_ref[...]) * g_ref[...]).astype(BF16)
    for c in range(0, 2 * MIX_WIDTH, 2 * LANES):
        y = _dot(h, wqk_ref[:, c:c + 2 * LANES])
        for s in range(2):
            col = c + s * LANES
            slab = y[:, s * LANES:(s + 1) * LANES]
            if col < MIX_WIDTH:
                q_ref[:, col:col + LANES] = (
                    _pair_norm(slab, qg_ref[...]) * (HEAD_DIM ** -0.5 * LOG2E)).astype(BF16)
            else:
                k_ref[:, col - MIX_WIDTH:col - MIX_WIDTH + LANES] = (
                    _pair_norm(slab, kg_ref[...]).astype(BF16))
    _store_vt(vt_ref, _nt_dot(wvt_ref[...], h).astype(BF16), tm)
    ym = _dot(h, wm_ref[...])
    for s in range(MEM_WIDTH // LANES):
        qm_ref[:, s * LANES:(s + 1) * LANES] = (
            _pair_norm(ym[:, s * LANES:(s + 1) * LANES], mg_ref[...]) * (HEAD_DIM ** -0.5 * LOG2E)
        ).astype(BF16)
    if has_gate:
        f = _dot(h, wf_ref[...]) + bf_ref[...]
        f_ref[...] = jnp.minimum(f, 0.0) - jnp.log1p(jnp.exp(-jnp.abs(f)))


def _qkv_proj(x2, seq, g, w_in, q_gain, k_gain, m_gain, b_f=None):
    t, d = x2.shape
    tm = ROW_TILE
    nblk = seq // KV_BLOCK
    per_seq = seq // tm
    has_gate = b_f is not None
    wqk = w_in[:, :2 * MIX_WIDTH].astype(BF16)
    wvt = w_in[:, 2 * MIX_WIDTH:3 * MIX_WIDTH].T.astype(BF16)
    wm = w_in[:, -MEM_WIDTH:].astype(BF16)
    pair = lambda v: jnp.tile(v, 2).reshape(1, LANES)
    const = lambda *shape: pl.BlockSpec(shape, lambda i: (0,) * len(shape))
    rows = lambda w: pl.BlockSpec((tm, w), lambda i: (i, 0))
    args = [x2, g.reshape(1, d), wqk, wvt, wm, pair(q_gain), pair(k_gain), pair(m_gain)]
    in_specs = [rows(d), const(1, d), const(d, 2 * MIX_WIDTH), const(MIX_WIDTH, d),
                const(d, MEM_WIDTH), const(1, LANES), const(1, LANES), const(1, LANES)]
    out_specs = [rows(MIX_WIDTH), rows(MIX_WIDTH),
                 pl.BlockSpec((1, tm // KV_BLOCK, MIX_WIDTH, KV_BLOCK),
                              lambda i: (i // per_seq, i % per_seq, 0, 0)),
                 rows(MEM_WIDTH)]
    out_shape = [jax.ShapeDtypeStruct((t, MIX_WIDTH), BF16), jax.ShapeDtypeStruct((t, MIX_WIDTH), BF16),
                 jax.ShapeDtypeStruct((t // seq, nblk, MIX_WIDTH, KV_BLOCK), BF16),
                 jax.ShapeDtypeStruct((t, MEM_WIDTH), BF16)]
    if has_gate:
        nh = N_MIX_HEADS
        wf = jnp.pad(w_in[:, 3 * MIX_WIDTH:3 * MIX_WIDTH + nh], ((0, 0), (0, LANES - nh))).astype(BF16)
        args += [wf, jnp.pad(b_f, (0, LANES - nh)).reshape(1, LANES)]
        in_specs += [const(d, LANES), const(1, LANES)]
        out_specs.append(rows(LANES))
        out_shape.append(jax.ShapeDtypeStruct((t, LANES), F32))
    return pl.pallas_call(
        functools.partial(_qkv_proj_kernel, has_gate=has_gate, tm=tm),
        grid=(t // tm,), in_specs=in_specs, out_specs=out_specs, out_shape=out_shape,
        compiler_params=_params(("parallel",)),
        name="qkv_proj_gate" if has_gate else "qkv_proj",
    )(*args)


def _cumsum_kernel(f_ref, o_ref, *, seq):
    r = lax.broadcasted_iota(jnp.int32, (KV_BLOCK, KV_BLOCK), 0)
    c = lax.broadcasted_iota(jnp.int32, (KV_BLOCK, KV_BLOCK), 1)
    tri = jnp.where(c <= r, 1.0, 0.0).astype(BF16)
    carry = jnp.zeros((1, LANES), F32)
    for j in range(seq // KV_BLOCK):
        x = f_ref[0, j * KV_BLOCK:(j + 1) * KV_BLOCK, :]
        x1 = x.astype(BF16)
        r1 = x - x1.astype(F32)
        x2 = r1.astype(BF16)
        x3 = (r1 - x2.astype(F32)).astype(BF16)
        cs = _dot(tri, x1) + _dot(tri, x2) + _dot(tri, x3) + carry
        o_ref[0, j * KV_BLOCK:(j + 1) * KV_BLOCK, :] = cs
        carry = cs[KV_BLOCK - 1:KV_BLOCK, :]


def _cumsum(flog):
    b, seq, w = flog.shape
    spec = pl.BlockSpec((1, seq, w), lambda i: (i, 0, 0))
    return pl.pallas_call(
        functools.partial(_cumsum_kernel, seq=seq),
        grid=(b,), in_specs=[spec], out_specs=spec,
        out_shape=jax.ShapeDtypeStruct((b, seq, w), F32),
        compiler_params=_params(("parallel",)),
        name="gate_cumsum",
    )(flog)


def _attn_kernel(*refs, mode, tq, sub, seq, n_sel):
    if mode == "fox":
        q_ref, k_ref, vt_ref, cq_ref, ck_ref, o_ref, m_sc, acc_sc, kaug_sc = refs
    elif mode == "moba":
        (q_ref, k_ref, vt_ref, pq_ref, pk_ref, sl_ref, o_ref,
         m_sc, acc_sc, kaug_sc, km_sc) = refs
    else:
        q_ref, k_ref, vt_ref, o_ref, m_sc, acc_sc = refs
    qi = pl.program_id(2)
    tk = KV_BLOCK
    nblk = seq // tk

    def spare(h):
        return HEAD_DIM * (1 - h)

    def own_lanes(h, shape):
        lane = lax.broadcasted_iota(jnp.int32, shape, 1)
        return (lane >= HEAD_DIM * h) & (lane < HEAD_DIM * (h + 1))

    def spare_slab(h, term, term_first, extra):
        n = term.shape[1]
        t1, t2, t3 = _split3(term)
        sl8 = lax.broadcasted_iota(jnp.int32, (8, n), 0)
        t_at, one_at = (0, 3) if term_first else (3, 0)
        head = jnp.where(sl8 == t_at, t1, jnp.where(sl8 == t_at + 1, t2, jnp.where(
            sl8 == t_at + 2, t3, jnp.where((sl8 >= one_at) & (sl8 < one_at + 3), 1.0, 0.0))))
        pieces = [head] if extra is None else [head, extra]
        used = sum(p.shape[0] for p in pieces)
        if spare(h):
            pieces.insert(0, jnp.zeros((spare(h), n), F32))
        pieces.append(jnp.zeros((LANES - spare(h) - used, n), F32))
        return jnp.concatenate(pieces, axis=0).T

    def key_side(h, col, first_key):
        n = col.shape[1]
        extra = None
        if mode == "moba":
            nbp = km_sc.shape[1]
            key = lax.broadcasted_iota(jnp.int32, (nbp, n), 1) + first_key
            blk = lax.shift_right_logical(key, tk.bit_length() - 1)
            extra = jnp.where(blk == lax.broadcasted_iota(jnp.int32, (nbp, n), 0), 1.0, 0.0)
        aug = spare_slab(h, col, True, extra)
        k = k_ref[0, first_key:first_key + n, :].astype(F32)
        return jnp.where(own_lanes(h, k.shape), k, aug).astype(BF16)

    chunk = 4 * tk
    if mode == "fox":
        @pl.when(qi == 0)
        def _():
            for h in range(2):
                for c in range(0, seq, chunk):
                    ck = ck_ref[0, 0, h:h + 1, c:c + chunk] * -LOG2E
                    kaug_sc[h, c:c + chunk, :] = key_side(h, ck, c)
    if mode == "moba":
        pos0 = pk_ref[0, :, 0:1]
        @pl.when(qi == 0)
        def _():
            for c in range(0, seq, chunk):
                pk = (pk_ref[0, :, c:c + chunk] - pos0).astype(F32)
                for h in range(2):
                    col = pk * (sl_ref[0, h:h + 1, 0:1] * LOG2E)
                    kaug_sc[h, c:c + chunk, :] = key_side(h, col, c)
            nbp = km_sc.shape[1]
            blk = lax.broadcasted_iota(jnp.int32, (nbp, seq), 0)
            key = lax.broadcasted_iota(jnp.int32, (nbp, seq), 1)
            pool = jnp.where((key >= blk * tk) & (key < (blk + 1) * tk), 1.0, 0.0).astype(BF16)
            km = _dot(pool, k_ref[0]) * (1.0 / tk)
            hi = km.astype(BF16)
            km_sc[0] = hi
            km_sc[1] = (km - hi.astype(F32)).astype(BF16)

    blk0 = qi * sub
    chains = [(i, h) for i in range(sub) for h in range(2)]

    def selection_bias(i, q_masked):
        nbp = km_sc.shape[1]
        n_iota = lax.broadcasted_iota(jnp.int32, (nbp, tq), 0)
        g = _nt_dot(km_sc[0], q_masked) + _nt_dot(km_sc[1], q_masked)
        g = jnp.where(n_iota < blk0 + i, g, -jnp.inf)
        bias = jnp.where(n_iota == blk0 + i, 0.0, MASKED)
        for _ in range(n_sel):
            mx = jnp.max(g, axis=0, keepdims=True)
            idx = jnp.min(jnp.where(g == mx, n_iota, nbp), axis=0, keepdims=True)
            hit = n_iota == idx
            ok = (mx > -jnp.inf) & (mx < jnp.inf)
            bias = jnp.where(hit & ok, 0.0, bias)
            g = jnp.where(hit, -jnp.inf, g)
        return bias

    def query_side(h, q_masked, row_term, sel):
        aug = spare_slab(h, row_term, False, sel)
        return jnp.where(own_lanes(h, aug.shape), q_masked, aug).astype(BF16)

    qh = {}
    for i in range(sub):
        qb = q_ref[0, i * tq:(i + 1) * tq, :]
        if mode == "mla":
            qh[i, 0], qh[i, 1] = qb[:, :LANES], qb[:, LANES:]
            continue
        qf = qb.astype(F32)
        for h in range(2):
            q_masked = jnp.where(own_lanes(h, qf.shape), qf, 0.0)
            if mode == "mem":
                qh[i, h] = q_masked.astype(BF16)
            elif mode == "fox":
                row_term = cq_ref[0, 0, h:h + 1, i * tq:(i + 1) * tq] * LOG2E
                qh[i, h] = query_side(h, q_masked, row_term, None)
            else:
                pq = (pq_ref[0, :, i * tq:(i + 1) * tq] - pos0).astype(F32)
                row_term = -((sl_ref[0, h:h + 1, 0:1] * LOG2E) * pq)
                sel = selection_bias(i, q_masked.astype(BF16)) if n_sel > 0 else None
                qh[i, h] = query_side(h, q_masked, row_term, sel)

    def logits(step):
        i, h, j, own = step
        start = j * tk if isinstance(j, int) else pl.multiple_of(j * tk, tk)
        if mode == "mla":
            kb = k_ref[0, pl.ds(start, tk), h * LANES:(h + 1) * LANES]
        elif mode == "mem":
            kb = k_ref[0, pl.ds(start, tk), :]
        else:
            kb = kaug_sc[h, pl.ds(start, tk), :]
        z = _nt_dot(kb, qh[i, h])
        if own and mode != "mem":
            key = lax.broadcasted_iota(jnp.int32, (tk, tq), 0)
            qry = lax.broadcasted_iota(jnp.int32, (tk, tq), 1)
            z = jnp.where(key <= qry, z, MASKED)
        return z

    ones_rows = jnp.ones((DENOM_ROWS, tk), BF16)

    def fold(step, z):
        i, h, j, own = step
        vt = jnp.concatenate([vt_ref[0, j, h * HEAD_DIM:(h + 1) * HEAD_DIM, :], ones_rows], axis=0)
        mz = jnp.max(z, axis=0, keepdims=True)
        if own:
            m_sc[i, h] = mz
            acc_sc[i, h] = _dot(vt, jnp.exp2(z - mz).astype(BF16))
        else:
            m_prev = m_sc[i, h]
            m_new = jnp.maximum(m_prev, mz)
            alpha = jnp.exp2(m_prev - m_new)
            acc_sc[i, h] = alpha * acc_sc[i, h] + _dot(vt, jnp.exp2(z - m_new).astype(BF16))
            m_sc[i, h] = m_new

    def run(steps):
        zs = {s: logits(steps[s]) for s in range(min(ATTN_LOOKAHEAD, len(steps)))}
        for s in range(len(steps)):
            if s + ATTN_LOOKAHEAD < len(steps):
                zs[s + ATTN_LOOKAHEAD] = logits(steps[s + ATTN_LOOKAHEAD])
            fold(steps[s], zs.pop(s))

    if mode == "mem":
        run([(i, h, 0, True) for i, h in chains])
    else:
        steps = [(i, h, blk0 + i, True) for i, h in chains]
        steps += [(i, h, blk0 + b, False) for b in range(sub - 1) for i, h in chains if i > b]
        run(steps)

        def body(g, carry):
            run([(i, h, g * sub + b, False) for b in range(sub) for i, h in chains])
            return carry
        lax.fori_loop(0, qi, body, 0)

    for i in range(sub):
        out_t = jnp.concatenate(
            [acc_sc[i, h, :HEAD_DIM, :] / acc_sc[i, h, HEAD_DIM:HEAD_DIM + 1, :] for h in range(2)],
            axis=0)
        o_ref[0, i * tq:(i + 1) * tq, :] = out_t.T.astype(BF16)


def _attention(mode, q, k, vt, extra=(), n_sel=0):
    b, s, _ = q.shape
    sk = k.shape[1]
    n_pairs = vt.shape[2] // LANES
    sub = ATTN_SUB
    tq = Q_TILE
    ts = sub * tq
    assert s % ts == 0
    qw = 2 * LANES if mode == "mla" else LANES
    in_specs = [pl.BlockSpec((1, ts, qw), lambda bi, p, i: (bi, i, p)),
                pl.BlockSpec((1, sk, qw), lambda bi, p, i: (bi, 0, p)),
                pl.BlockSpec((1, sk // KV_BLOCK, LANES, KV_BLOCK), lambda bi, p, i: (bi, 0, p, 0))]
    scratch = [pltpu.VMEM((sub, 2, 1, tq), F32),
               pltpu.VMEM((sub, 2, HEAD_DIM + DENOM_ROWS, tq), F32)]
    if mode == "fox":
        in_specs += [pl.BlockSpec((1, 1, 2, ts), lambda bi, p, i: (bi, p, 0, i)),
                     pl.BlockSpec((1, 1, 2, s), lambda bi, p, i: (bi, p, 0, 0))]
        scratch += [pltpu.VMEM((2, s, LANES), BF16)]
    elif mode == "moba":
        nbp = -(-(s // KV_BLOCK) // 16) * 16
        in_specs += [pl.BlockSpec((1, 1, ts), lambda bi, p, i: (bi, 0, i)),
                     pl.BlockSpec((1, 1, s), lambda bi, p, i: (bi, 0, 0)),
                     pl.BlockSpec((1, 2, LANES), lambda bi, p, i: (p, 0, 0))]
        assert SEL_LANE + nbp <= HEAD_DIM
        scratch += [pltpu.VMEM((2, s, LANES), BF16), pltpu.VMEM((2, nbp, LANES), BF16)]
    return pl.pallas_call(
        functools.partial(_attn_kernel, mode=mode, tq=tq, sub=sub, seq=sk, n_sel=n_sel),
        grid=(b, n_pairs, s // ts),
        in_specs=in_specs,
        out_specs=pl.BlockSpec((1, ts, LANES), lambda bi, p, i: (bi, i, p)),
        out_shape=jax.ShapeDtypeStruct((b, s, n_pairs * LANES), BF16),
        scratch_shapes=scratch,
        compiler_params=_params(("parallel", "parallel", "arbitrary")),
        name="attn_" + mode,
    )(q, k, vt, *extra)


def _rope_table_kernel(pos_ref, invf_ref, cos_ref, sin_ref):
    ang = pos_ref[...].astype(F32) * invf_ref[...]
    cos_ref[...] = jnp.cos(ang)
    sin_ref[...] = jnp.sin(ang)


def _rope_tables(positions):
    t = positions.size
    half = MLA_ROPE_DIM // 2
    per_row = LANES // half
    inv_freq = ROPE_THETA ** (-jnp.arange(0, MLA_ROPE_DIM, 2, dtype=F32) / MLA_ROPE_DIM)
    pos = jnp.repeat(positions.reshape(t // per_row, per_row), half, axis=1)
    rows = min(t // per_row, 1024)
    spec = pl.BlockSpec((rows, LANES), lambda i: (i, 0))
    cos, sin = pl.pallas_call(
        _rope_table_kernel, grid=(t // per_row // rows,),
        in_specs=[spec, pl.BlockSpec((1, LANES), lambda i: (0, 0))],
        out_specs=[spec, spec],
        out_shape=[jax.ShapeDtypeStruct((t // per_row, LANES), F32)] * 2,
        compiler_params=_params(("parallel",)),
        name="rope_tables",
    )(pos, jnp.tile(inv_freq, per_row).reshape(1, LANES))
    pad_r = LANES - MLA_NOPE_DIM - MLA_ROPE_DIM
    slab = lambda v, fill: jnp.concatenate(
        [jnp.full((t, MLA_NOPE_DIM), fill, F32), v.reshape(t, half), v.reshape(t, half),
         jnp.full((t, pad_r), fill, F32)], axis=1)
    return slab(cos, 1.0), slab(sin, 0.0)


def _mla_proj_kernel(x_ref, cos_ref, sin_ref, g_ref, w1_ref, qa_ref, kva_ref, wq_ref, wqr_ref, wk_ref,
                     wvt_ref, qg_ref, kg_ref, mg_ref, q_ref, k_ref, vt_ref, qm_ref, *, tm):
    h = (_rms(x_ref[...]) * g_ref[...]).astype(BF16)
    p = _dot(h, w1_ref[...])
    o = MLA_Q_RANK
    q_lat = p[:, :o]
    kv_lat = p[:, o:o + LANES]
    kr = p[:, o + LANES:o + 2 * LANES]
    kr_rot = p[:, o + 2 * LANES:o + 3 * LANES]
    ym = p[:, o + 3 * LANES:]
    qn = (_rms(q_lat) * qa_ref[...]).astype(BF16)
    kvn = (_rms(kv_lat) * kva_ref[...]).astype(BF16)
    cos, sin = cos_ref[...], sin_ref[...]
    k_rope = kr * cos + kr_rot * sin
    width = N_MIX_HEADS * LANES
    for c in range(0, width, 2 * LANES):
        qc = _dot(qn, wq_ref[:, c:c + 2 * LANES])
        qr = _dot(qn, wqr_ref[:, c:c + 2 * LANES])
        kc = _dot(kvn, wk_ref[:, c:c + 2 * LANES])
        for s in range(2):
            sl = slice(s * LANES, (s + 1) * LANES)
            out = slice(c + s * LANES, c + (s + 1) * LANES)
            qq = qc[:, sl] * cos + qr[:, sl] * sin
            q_ref[:, out] = (_slab_norm(qq, qg_ref[...], MLA_QK_DIM)
                             * (MLA_QK_DIM ** -0.5 * LOG2E)).astype(BF16)
            k_ref[:, out] = _slab_norm(kc[:, sl] + k_rope, kg_ref[...], MLA_QK_DIM).astype(BF16)
    _store_vt(vt_ref, _nt_dot(wvt_ref[...], kvn).astype(BF16), tm)
    for s in range(MEM_WIDTH // LANES):
        qm_ref[:, s * LANES:(s + 1) * LANES] = (
            _pair_norm(ym[:, s * LANES:(s + 1) * LANES], mg_ref[...]) * (HEAD_DIM ** -0.5 * LOG2E)
        ).astype(BF16)


def _rot_half(w):
    half = MLA_ROPE_DIM // 2
    return jnp.concatenate([-w[..., half:], w[..., :half]], axis=-1)


def _mla_proj(x2, cos, sin, seq, g, w_in, qa, kva, w_q_up, w_kv_up, q_gain, k_gain, m_gain):
    t, d = x2.shape
    tm = ROW_TILE
    per_seq = seq // tm
    nh, nope, rope_d = N_MIX_HEADS, MLA_NOPE_DIM, MLA_ROPE_DIM
    pad_r = LANES - nope - rope_d
    o = MLA_Q_RANK + MLA_KV_RANK
    k_r = w_in[:, o:o + rope_d]
    slab = lambda w: jnp.pad(w, ((0, 0), (nope, pad_r)))
    w1 = jnp.concatenate([w_in[:, :o], slab(k_r), slab(_rot_half(k_r)), w_in[:, -MEM_WIDTH:]],
                         axis=1).astype(BF16)
    wq3 = w_q_up.reshape(MLA_Q_RANK, nh, MLA_QK_DIM)
    wq = jnp.pad(wq3, ((0, 0), (0, 0), (0, pad_r))).reshape(MLA_Q_RANK, nh * LANES).astype(BF16)
    wqr = jnp.pad(_rot_half(wq3[..., nope:]), ((0, 0), (0, 0), (nope, pad_r))
                  ).reshape(MLA_Q_RANK, nh * LANES).astype(BF16)
    wkv3 = w_kv_up.reshape(MLA_KV_RANK, nh, nope + MLA_V_DIM)
    wk = jnp.pad(wkv3[..., :nope], ((0, 0), (0, 0), (0, LANES - nope))
                 ).reshape(MLA_KV_RANK, nh * LANES).astype(BF16)
    wvt = wkv3[..., nope:].reshape(MLA_KV_RANK, nh * MLA_V_DIM).T.astype(BF16)
    padg = lambda v: jnp.pad(v, (0, pad_r)).reshape(1, LANES)
    const = lambda *shape: pl.BlockSpec(shape, lambda i: (0,) * len(shape))
    rows = lambda w: pl.BlockSpec((tm, w), lambda i: (i, 0))
    qkw = nh * LANES
    return pl.pallas_call(
        functools.partial(_mla_proj_kernel, tm=tm),
        grid=(t // tm,),
        in_specs=[rows(d), rows(LANES), rows(LANES), const(1, d), const(d, w1.shape[1]),
                  const(1, MLA_Q_RANK), const(1, MLA_KV_RANK), const(MLA_Q_RANK, qkw),
                  const(MLA_Q_RANK, qkw), const(MLA_KV_RANK, qkw), const(MIX_WIDTH, MLA_KV_RANK),
                  const(1, LANES), const(1, LANES), const(1, LANES)],
        out_specs=[rows(qkw), rows(qkw),
                   pl.BlockSpec((1, tm // KV_BLOCK, MIX_WIDTH, KV_BLOCK),
                                lambda i: (i // per_seq, i % per_seq, 0, 0)),
                   rows(MEM_WIDTH)],
        out_shape=[jax.ShapeDtypeStruct((t, qkw), BF16), jax.ShapeDtypeStruct((t, qkw), BF16),
                   jax.ShapeDtypeStruct((t // seq, seq // KV_BLOCK, MIX_WIDTH, KV_BLOCK), BF16),
                   jax.ShapeDtypeStruct((t, MEM_WIDTH), BF16)],
        compiler_params=_params(("parallel",)),
        name="mla_proj",
    )(x2, cos, sin, g.reshape(1, d), w1, qa.reshape(1, -1), kva.reshape(1, -1), wq, wqr, wk, wvt,
      padg(q_gain), padg(k_gain), jnp.tile(m_gain, 2).reshape(1, LANES))


def _ffn_kernel(x_ref, xh_ref, om_ref, omh_ref, oc_ref, och_ref, wo1_ref, wo2_ref, g_ref, wup_ref,
                cw_ref, cb_ref, wdn_ref, o_ref, u_sc, act_sc, *, tm, per_seq):
    i = pl.program_id(0)
    cat = lambda halo_ref, main_ref: jnp.concatenate([halo_ref[...], main_ref[...]], axis=0)
    x_all = (cat(xh_ref, x_ref) + _dot(cat(omh_ref, om_ref), wo1_ref[...])
             + _dot(cat(och_ref, oc_ref), wo2_ref[...]))
    x = x_all[HALO:]
    keep = jnp.where(i % per_seq == 0, 0.0, 1.0)
    row = lax.broadcasted_iota(jnp.int32, (HALO + tm, 1), 0)
    h_all = (_rms(x_all) * g_ref[...] * jnp.where(row < HALO, keep, 1.0)).astype(BF16)
    tf = FF_CHUNK
    n_chunks = D_FF // tf

    def columns(c, part):
        return slice(part * D_FF + c * tf, part * D_FF + (c + 1) * tf)

    def up(c):
        for part in range(2):
            u_sc[c % 2, part] = _dot(h_all, wup_ref[:, columns(c, part)])

    def activate(c):
        halves = []
        for part in range(2):
            cw = cw_ref[:, columns(c, part)]
            u = u_sc.at[c % 2, part]
            halves.append(cw[2:3] * u[HALO:HALO + tm, :] + cw[1:2] * u[HALO - 1:HALO - 1 + tm, :]
                          + cw[0:1] * u[HALO - 2:HALO - 2 + tm, :] + cb_ref[:, columns(c, part)])
        gate, val = halves
        act_sc[:, c * tf:(c + 1) * tf] = (gate * jax.nn.sigmoid(gate) * val).astype(BF16)

    up(0)
    for c in range(n_chunks):
        if c + 1 < n_chunks:
            up(c + 1)
        activate(c)
    o_ref[...] = x + _dot(act_sc[...], wdn_ref[...])


def _out_proj_ffn(x2, o_mix, o_mem, seq, layer, w_out, g, w_up, conv_w, conv_b, w_down):
    t, d = x2.shape
    tm = ROW_TILE
    per_seq = seq // tm
    rows = lambda w: pl.BlockSpec((tm, w), lambda i: (i, 0))
    halo = lambda w: pl.BlockSpec((HALO, w), lambda i: (jnp.maximum(i * (tm // HALO) - 1, 0), 0))
    resident = lambda r, c, blk=0: pl.BlockSpec((None, r, c), lambda i: (layer, blk, 0),
                                                pipeline_mode=pl.Buffered(1))
    return pl.pallas_call(
        functools.partial(_ffn_kernel, tm=tm, per_seq=per_seq),
        grid=(t // tm,),
        in_specs=[rows(d), halo(d), rows(MIX_WIDTH), halo(MIX_WIDTH), rows(MEM_WIDTH), halo(MEM_WIDTH),
                  resident(MIX_WIDTH, d), resident(MEM_WIDTH, d, MIX_WIDTH // MEM_WIDTH),
                  resident(1, d), resident(d, 2 * D_FF), resident(3, 2 * D_FF),
                  resident(1, 2 * D_FF), resident(D_FF, d)],
        out_specs=rows(d), out_shape=jax.ShapeDtypeStruct((t, d), F32),
        scratch_shapes=[pltpu.VMEM((2, 2, HALO + tm, FF_CHUNK), F32), pltpu.VMEM((tm, D_FF), BF16)],
        compiler_params=_params(("parallel",)),
        name="out_proj_conv_ffn",
    )(x2, x2, o_mix, o_mix, o_mem, o_mem, w_out, w_out, g, w_up, conv_w, conv_b, w_down)


def kernel(x, mem, positions, norm_mix, norm_ffn, w_out, mem_norm, w_mem_kv, mem_q_gain,
           mem_k_gain, fox_w_in, fox_b_f, fox_q_gain, fox_k_gain, mla_w_in, mla_qa_norm,
           mla_kva_norm, mla_w_q_up, mla_w_kv_up, mla_q_gain, mla_k_gain, moba_w_in,
           moba_q_gain, moba_k_gain, ffn_w_up, ffn_conv_w, ffn_conv_b, ffn_w_down):
    b, s, d = x.shape
    depth = norm_mix.shape[0]
    assert d == D_MODEL and s % ROW_TILE == 0 and mem.shape[1] == KV_BLOCK
    n_pairs = N_MIX_HEADS // 2
    k_mem, vt_mem = _mem_prep(mem, mem_norm, w_mem_kv, mem_k_gain)
    x2 = x.reshape(b * s, d)
    rope_cos, rope_sin = _rope_tables(positions) if depth > 1 else (None, None)
    n_blocks = s // MOBA_BLOCK
    slopes = 2.0 ** (-8.0 * jnp.arange(1, N_MIX_HEADS + 1, dtype=F32) / N_MIX_HEADS)
    slopes = jnp.broadcast_to(slopes.reshape(n_pairs, 2, 1), (n_pairs, 2, LANES))

    ffn_weights = (w_out.astype(BF16), norm_ffn.reshape(depth, 1, d), ffn_w_up.astype(BF16),
                   ffn_conv_w, ffn_conv_b.reshape(depth, 1, -1), ffn_w_down.astype(BF16))

    for i in range(depth):
        kind, j = i % N_MIXERS, i // N_MIXERS
        if kind == 0:
            q, k, vt, qm, flog = _qkv_proj(x2, s, norm_mix[i], fox_w_in[j], fox_q_gain[j],
                                           fox_k_gain[j], mem_q_gain[i], fox_b_f[j])
            cum = _cumsum(flog.reshape(b, s, LANES))[..., :N_MIX_HEADS]
            cum = cum.transpose(0, 2, 1).reshape(b, n_pairs, 2, s)
            o_mix = _attention("fox", q.reshape(b, s, -1), k.reshape(b, s, -1), vt, (cum, cum))
        elif kind == 1:
            q, k, vt, qm = _mla_proj(x2, rope_cos, rope_sin, s, norm_mix[i], mla_w_in[j], mla_qa_norm[j],
                                     mla_kva_norm[j], mla_w_q_up[j], mla_w_kv_up[j],
                                     mla_q_gain[j], mla_k_gain[j], mem_q_gain[i])
            o_mix = _attention("mla", q.reshape(b, s, -1), k.reshape(b, s, -1), vt)
        else:
            q, k, vt, qm = _qkv_proj(x2, s, norm_mix[i], moba_w_in[j], moba_q_gain[j],
                                     moba_k_gain[j], mem_q_gain[i])
            extra = (positions.reshape(b, 1, s), positions.reshape(b, 1, s), slopes)
            o_mix = _attention("moba", q.reshape(b, s, -1), k.reshape(b, s, -1), vt, extra,
                               n_sel=min(MOBA_TOPK, n_blocks - 1))
        o_mem = _attention("mem", qm.reshape(b, s, MEM_WIDTH), k_mem[i], vt_mem)
        x2 = _out_proj_ffn(x2, o_mix.reshape(b * s, MIX_WIDTH), o_mem.reshape(b * s, MEM_WIDTH), s,
                           i, *ffn_weights)
    return x2.reshape(b, s, d)
```
